```python
import math
import functools
import jax
import jax.numpy as jnp
from jax import lax
import numpy as np

D_MODEL = 2048
BATCH = 4
SEQ = 4096
DEPTH = 1
DEC_BATCH = 128
DEC_SEQ = 8
PAST_LEN = 16384
PAGE_SIZE = 128

A_HEADS = 8
A_NOPE = 128
A_ROPE = 64
A_V = 128
KV_LORA = 512
ROPE_THETA = 10000.0
B_HEADS = 8
B_KV_HEADS = 2
B_HEAD_DIM = 128
IDX_HEADS = 16
IDX_DIM = 64
TOPK_MAX = 256
REL_BUCKETS = 32
REL_MAX_DIST = 128
N_GROUPS = 4
EXPERTS_PER_GROUP = 8
N_EXPERTS = N_GROUPS * EXPERTS_PER_GROUP
TOP_K_EXPERTS = 2
D_EXPERT = 512
Q_BLOCK = 128
EPS = 1e-6
IN_SIZES = (A_HEADS * (A_NOPE + A_ROPE), KV_LORA, A_ROPE,
            B_HEADS * B_HEAD_DIM, B_KV_HEADS * B_HEAD_DIM, B_KV_HEADS * B_HEAD_DIM,
            IDX_HEADS * IDX_DIM, IDX_DIM, IDX_HEADS, D_MODEL, D_MODEL)
IN_COLS = sum(IN_SIZES)

kernel_name = 'hybrid_mla_dsa_hmoe_step'


def rmsnorm(x, g):
    xf = x.astype(jnp.float32)
    y = xf * lax.rsqrt(jnp.mean(xf * xf, axis=-1, keepdims=True) + EPS)
    return (y * g.astype(jnp.float32)).astype(x.dtype)


def rope(x, pos):
    half = A_ROPE // 2
    inv = ROPE_THETA ** (-jnp.arange(half, dtype=jnp.float32) / half)
    ang = pos.astype(jnp.float32)[:, None] * inv[None, :]
    ang = ang.reshape((ang.shape[0],) + (1,) * (x.ndim - 3) + (half,))
    cos = jnp.cos(ang).astype(x.dtype)
    sin = jnp.sin(ang).astype(x.dtype)
    x1, x2 = x[..., :half], x[..., half:]
    return jnp.concatenate([x1 * cos - x2 * sin, x1 * sin + x2 * cos], axis=-1)


def t5_bucket(dist):
    n = jnp.maximum(dist, 0)
    max_exact = REL_BUCKETS // 2
    nf = jnp.maximum(n, 1).astype(jnp.float32)
    large = max_exact + (jnp.log(nf / max_exact) / math.log(REL_MAX_DIST / max_exact)
                         * (REL_BUCKETS - max_exact)).astype(jnp.int32)
    large = jnp.minimum(large, REL_BUCKETS - 1)
    return jnp.where(n < max_exact, n, large)


def adaln(c, w_ada, b_ada):
    mod = jax.nn.silu(c) @ w_ada + b_ada
    return jnp.split(mod[:, None, :], 6, axis=-1)


def modulate(x, g, shift, scale):
    return rmsnorm(x, g) * (1 + scale) + shift


def mixer_inputs(h, pos, w_in, kv_norm, kidx_norm):
    B_, S_ = h.shape[0], h.shape[1]
    offs = np.cumsum(np.array(IN_SIZES))[:-1].tolist()
    q_a, ckv, kr, q_b, k_b, v_b, q_i, k_i, w_i, g_a, g_b = jnp.split(h @ w_in, offs, axis=-1)
    q_a = q_a.reshape(B_, S_, A_HEADS, A_NOPE + A_ROPE)
    qa_nope = q_a[..., :A_NOPE]
    qa_rope = rope(q_a[..., A_NOPE:], pos)
    ckv = rmsnorm(ckv, kv_norm)
    kr = rope(kr, pos)
    q_b = q_b.reshape(B_, S_, B_KV_HEADS, B_HEADS // B_KV_HEADS, B_HEAD_DIM)
    k_b = k_b.reshape(B_, S_, B_KV_HEADS, B_HEAD_DIM)
    v_b = v_b.reshape(B_, S_, B_KV_HEADS, B_HEAD_DIM)
    q_i = q_i.reshape(B_, S_, IDX_HEADS, IDX_DIM)
    k_i = rmsnorm(k_i, kidx_norm)
    w_i = w_i * IDX_HEADS ** -0.5
    return (qa_nope, qa_rope, ckv, kr, q_b, k_b, v_b, q_i, w_i, k_i, g_a, g_b)


def indexer_scores(q_i, w_i, k_i):
    s = jax.nn.relu(jnp.einsum('bthd,bsd->bths', q_i, k_i) * IDX_DIM ** -0.5)
    return jnp.einsum('bths,bth->bts', s, w_i).astype(jnp.float32)


def sparse_attend(q_b, k_sel, v_sel, valid, idx, qpos, rel_bias):
    B_, T_, K_ = idx.shape
    s = jnp.einsum('btgrd,btkgd->btgrk', q_b, k_sel).astype(jnp.float32) * B_HEAD_DIM ** -0.5
    bias = rel_bias[t5_bucket(qpos[None, :, None] - idx)].astype(jnp.float32)
    bias = bias.reshape(B_, T_, K_, B_KV_HEADS, B_HEADS // B_KV_HEADS).transpose(0, 1, 3, 4, 2)
    s = jnp.where(valid[:, :, None, None, :], s + bias, -jnp.inf)
    p = jax.nn.softmax(s, axis=-1).astype(v_sel.dtype)
    o = jnp.einsum('btgrk,btkgd->btgrd', p, v_sel)
    return o.reshape(B_, T_, B_HEADS * B_HEAD_DIM)


def mla_prompt(qa_nope, qa_rope, ckv, kr, w_uk, w_uv):
    B_, S_ = ckv.shape[0], ckv.shape[1]
    nb = S_ // Q_BLOCK
    scale = (A_NOPE + A_ROPE) ** -0.5
    k_nope = jnp.einsum('bsc,chd->bshd', ckv, w_uk)
    v = jnp.einsum('bsc,chd->bshd', ckv, w_uv)
    kpos = jnp.arange(S_)

    def blk(args):
        i, qn, qr = args
        qpos = i * Q_BLOCK + jnp.arange(Q_BLOCK)
        s = (jnp.einsum('bthd,bshd->bhts', qn, k_nope)
             + jnp.einsum('bthr,bsr->bhts', qr, kr)).astype(jnp.float32) * scale
        s = jnp.where((kpos[None, :] <= qpos[:, None])[None, None], s, -jnp.inf)
        p = jax.nn.softmax(s, axis=-1).astype(v.dtype)
        return jnp.einsum('bhts,bshd->bthd', p, v).reshape(B_, Q_BLOCK, A_HEADS * A_V)

    blocks = lambda a: a.reshape((B_, nb, Q_BLOCK) + a.shape[2:]).swapaxes(0, 1)
    out = lax.map(blk, (jnp.arange(nb), blocks(qa_nope), blocks(qa_rope)))
    return out.swapaxes(0, 1).reshape(B_, S_, A_HEADS * A_V)


def dsa_prompt(q_b, k_b, v_b, q_i, w_i, k_i, rel_bias):
    B_, S_ = k_b.shape[0], k_b.shape[1]
    nb = S_ // Q_BLOCK
    n_sel = min(TOPK_MAX, S_ // 4)
    kpos = jnp.arange(S_)
    gather_rows = jax.vmap(lambda kk, ii: kk[ii])

    def blk(args):
        i, qb, qi, wi = args
        qpos = i * Q_BLOCK + jnp.arange(Q_BLOCK)
        sc = indexer_scores(qi, wi, k_i)
        sc = jnp.where((kpos[None, :] <= qpos[:, None])[None], sc, -jnp.inf)
        _, idx = lax.top_k(sc, n_sel)
        valid = idx <= qpos[None, :, None]
        return sparse_attend(qb, gather_rows(k_b, idx), gather_rows(v_b, idx), valid, idx, qpos, rel_bias)

    blocks = lambda a: a.reshape((B_, nb, Q_BLOCK) + a.shape[2:]).swapaxes(0, 1)
    out = lax.map(blk, (jnp.arange(nb), blocks(q_b), blocks(q_i), blocks(w_i)))
    return out.swapaxes(0, 1).reshape(B_, S_, B_HEADS * B_HEAD_DIM)


def mla_sample(qa_nope, qa_rope, ckv_new, kr_new, cache_ckv, cache_krope, page_table, layer, w_uk, w_uv):
    B_, T_ = ckv_new.shape[0], ckv_new.shape[1]
    scale = (A_NOPE + A_ROPE) ** -0.5
    q_lat = jnp.einsum('bthd,chd->bthc', qa_nope, w_uk)

    def scores(ck, krr):
        return (jnp.einsum('bthc,bsc->bths', q_lat, ck)
                + jnp.einsum('bthr,bsr->bths', qa_rope, krr)).astype(jnp.float32) * scale

    tq = jnp.arange(T_)
    s_own = jnp.where((tq[None, :] <= tq[:, None])[None, :, None, :], scores(ckv_new, kr_new), -jnp.inf)
    m = jnp.max(s_own, axis=-1)
    p = jnp.exp(s_own - m[..., None])
    l = jnp.sum(p, axis=-1)
    acc = jnp.einsum('bths,bsc->bthc', p, ckv_new.astype(jnp.float32))

    def step(carry, pages):
        m, l, acc = carry
        ck = cache_ckv[layer, pages]
        krr = cache_krope[layer, pages]
        s = scores(ck, krr)
        m_new = jnp.maximum(m, jnp.max(s, axis=-1))
        corr = jnp.exp(m - m_new)
        p = jnp.exp(s - m_new[..., None])
        l = l * corr + jnp.sum(p, axis=-1)
        acc = acc * corr[..., None] + jnp.einsum('bths,bsc->bthc', p, ck.astype(jnp.float32))
        return (m_new, l, acc), None

    (m, l, acc), _ = lax.scan(step, (m, l, acc), page_table.T)
    o_lat = (acc / l[..., None]).astype(ckv_new.dtype)
    return jnp.einsum('bthc,chd->bthd', o_lat, w_uv).reshape(B_, T_, A_HEADS * A_V)


def dsa_sample(q_b, k_new, v_new, q_i, w_i, k_i_new, cache_k, cache_v, cache_kidx, page_table, layer, rel_bias):
    B_, T_ = k_new.shape[0], k_new.shape[1]
    n_pages = page_table.shape[1]
    past = n_pages * PAGE_SIZE
    n_sel = min(TOPK_MAX, (past + T_) // 4)

    def step(_, pages):
        return None, indexer_scores(q_i, w_i, cache_kidx[layer, pages])

    _, sc_past = lax.scan(step, None, page_table.T)
    sc_past = sc_past.transpose(1, 2, 0, 3).reshape(B_, T_, past)
    tq = jnp.arange(T_)
    sc_new = jnp.where((tq[None, :] <= tq[:, None])[None], indexer_scores(q_i, w_i, k_i_new), -jnp.inf)
    _, idx = lax.top_k(jnp.concatenate([sc_past, sc_new], axis=-1), n_sel)
    is_new = (idx >= past)[..., None, None]
    page = jnp.minimum(idx // PAGE_SIZE, n_pages - 1)
    phys = jax.vmap(lambda pt, pg: pt[pg])(page_table, page)
    off = idx % PAGE_SIZE
    j = jnp.clip(idx - past, 0, T_ - 1)
    gather_new = jax.vmap(lambda kk, jj: kk[jj])
    k_sel = jnp.where(is_new, gather_new(k_new, j), cache_k[layer, phys, off])
    v_sel = jnp.where(is_new, gather_new(v_new, j), cache_v[layer, phys, off])
    qpos = past + tq
    valid = idx <= qpos[None, :, None]
    return sparse_attend(q_b, k_sel, v_sel, valid, idx, qpos, rel_bias)


def prompt_mix(w_uk, w_uv, rel_bias, qa_nope, qa_rope, ckv, kr, q_b, k_b, v_b, q_i, w_i, k_i):
    return (mla_prompt(qa_nope, qa_rope, ckv, kr, w_uk, w_uv),
            dsa_prompt(q_b, k_b, v_b, q_i, w_i, k_i, rel_bias))


def sample_mix(cache_ckv, cache_krope, cache_k, cache_v, cache_kidx, page_table, layer, w_uk, w_uv, rel_bias,
               qa_nope, qa_rope, ckv, kr, q_b, k_b, v_b, q_i, w_i, k_i):
    return (mla_sample(qa_nope, qa_rope, ckv, kr, cache_ckv, cache_krope, page_table, layer, w_uk, w_uv),
            dsa_sample(q_b, k_b, v_b, q_i, w_i, k_i, cache_k, cache_v, cache_kidx, page_table, layer, rel_bias))


def hier_moe(h, w_grp, b_grp, w_er, b_er, w1, w3, w2):
    B_, S_, D_ = h.shape
    xt = h.reshape(-1, D_)
    glog = (xt @ w_grp + b_grp).astype(jnp.float32)
    g_sel = jnp.argmax(glog, axis=-1)
    g_w = jnp.take_along_axis(jax.nn.softmax(glog, axis=-1), g_sel[:, None], axis=-1)
    elog = (xt @ w_er + b_er).astype(jnp.float32).reshape(-1, N_GROUPS, EXPERTS_PER_GROUP)
    elog_g = jnp.take_along_axis(elog, g_sel[:, None, None], axis=1)[:, 0]
    top_v, top_i = lax.top_k(elog_g, TOP_K_EXPERTS)
    top_w = jax.nn.softmax(top_v, axis=-1) * g_w
    expert_id = g_sel[:, None] * EXPERTS_PER_GROUP + top_i
    gates = jnp.sum(jax.nn.one_hot(expert_id, N_EXPERTS, dtype=jnp.float32) * top_w[..., None], axis=1)
    gates = gates.astype(h.dtype)

    def expert_step(acc, ew):
        a, b, c, g = ew
        hid = jax.nn.silu(xt @ a) * (xt @ b)
        return acc + g[:, None] * (hid @ c), None

    y, _ = lax.scan(expert_step, jnp.zeros_like(xt), (w1, w3, w2, gates.T))
    return y.reshape(B_, S_, D_)


def trunk_layer(x, c, pos, mix_fn, w_in, kv_norm, kidx_norm, w_o_a, w_o_b, w_out, norm_attn, norm_ffn,
                w_ada, b_ada, w_grp, b_grp, w_er, b_er, w1, w3, w2):
    sh_a, sc_a, gt_a, sh_f, sc_f, gt_f = adaln(c, w_ada, b_ada)
    h = modulate(x, norm_attn, sh_a, sc_a)
    (qa_nope, qa_rope, ckv, kr, q_b, k_b, v_b, q_i, w_i, k_i, g_a, g_b) = mixer_inputs(h, pos, w_in, kv_norm, kidx_norm)
    o_a, o_b = mix_fn(qa_nope, qa_rope, ckv, kr, q_b, k_b, v_b, q_i, w_i, k_i)
    merged = jax.nn.sigmoid(g_a) * (o_a @ w_o_a) + jax.nn.sigmoid(g_b) * (o_b @ w_o_b)
    x = x + gt_a * (merged @ w_out)
    h = modulate(x, norm_ffn, sh_f, sc_f)
    x = x + gt_f * hier_moe(h, w_grp, b_grp, w_er, b_er, w1, w3, w2)
    return x, (ckv, kr, k_b, v_b, k_i)


def setup_inputs(seed: int = 0) -> dict:
    key = jax.random.key(seed)
    keys = jax.random.split(key, 40)
    counter = [0]

    def nk():
        kk = keys[counter[0]]
        counter[0] += 1
        return kk

    def nrm(shape, scale):
        return jax.random.normal(nk(), shape, jnp.float32) * scale

    n_pages = PAST_LEN // PAGE_SIZE
    n_used = DEC_BATCH * n_pages
    n_pool = n_used + n_used // 4
    page_table = jax.random.permutation(nk(), n_pool)[:n_used].reshape(DEC_BATCH, n_pages).astype(jnp.int32)
    return {
        'x_prompt': nrm((BATCH, SEQ, D_MODEL), 1.0),
        'x_sample': nrm((DEC_BATCH, DEC_SEQ, D_MODEL), 1.0),
        'c_prompt': nrm((BATCH, D_MODEL), 1.0),
        'c_sample': nrm((DEC_BATCH, D_MODEL), 1.0),
        'cache_ckv': nrm((DEPTH, n_pool, PAGE_SIZE, KV_LORA), 1.0),
        'cache_krope': nrm((DEPTH, n_pool, PAGE_SIZE, A_ROPE), 1.0),
        'cache_k': nrm((DEPTH, n_pool, PAGE_SIZE, B_KV_HEADS, B_HEAD_DIM), 1.0),
        'cache_v': nrm((DEPTH, n_pool, PAGE_SIZE, B_KV_HEADS, B_HEAD_DIM), 1.0),
        'cache_kidx': nrm((DEPTH, n_pool, PAGE_SIZE, IDX_DIM), 1.0),
        'page_table': page_table,
        'w_in': nrm((DEPTH, D_MODEL, IN_COLS), D_MODEL ** -0.5),
        'kv_norm': 1.0 + nrm((DEPTH, KV_LORA), 0.02),
        'kidx_norm': 1.0 + nrm((DEPTH, IDX_DIM), 0.02),
        'w_uk': nrm((DEPTH, KV_LORA, A_HEADS, A_NOPE), KV_LORA ** -0.5),
        'w_uv': nrm((DEPTH, KV_LORA, A_HEADS, A_V), KV_LORA ** -0.5),
        'w_o_a': nrm((DEPTH, A_HEADS * A_V, D_MODEL), (A_HEADS * A_V) ** -0.5),
        'w_o_b': nrm((DEPTH, B_HEADS * B_HEAD_DIM, D_MODEL), (B_HEADS * B_HEAD_DIM) ** -0.5),
        'w_out': nrm((DEPTH, D_MODEL, D_MODEL), D_MODEL ** -0.5),
        'rel_bias': nrm((REL_BUCKETS, B_HEADS), 0.5),
        'norm_attn': 1.0 + nrm((DEPTH, D_MODEL), 0.02),
        'norm_ffn': 1.0 + nrm((DEPTH, D_MODEL), 0.02),
        'norm_final': 1.0 + nrm((D_MODEL,), 0.02),
        'w_ada': nrm((DEPTH, D_MODEL, 6 * D_MODEL), 0.5 * D_MODEL ** -0.5),
        'b_ada': nrm((DEPTH, 6 * D_MODEL), 0.02),
        'w_grp': nrm((DEPTH, D_MODEL, N_GROUPS), D_MODEL ** -0.5),
        'b_grp': nrm((DEPTH, N_GROUPS), 0.01),
        'w_er': nrm((DEPTH, D_MODEL, N_EXPERTS), D_MODEL ** -0.5),
        'b_er': nrm((DEPTH, N_EXPERTS), 0.01),
        'w1': nrm((DEPTH, N_EXPERTS, D_MODEL, D_EXPERT), D_MODEL ** -0.5),
        'w3': nrm((DEPTH, N_EXPERTS, D_MODEL, D_EXPERT), D_MODEL ** -0.5),
        'w2': nrm((DEPTH, N_EXPERTS, D_EXPERT, D_MODEL), D_EXPERT ** -0.5),
    }


def reference(x_prompt, x_sample, c_prompt, c_sample, cache_ckv, cache_krope, cache_k, cache_v, cache_kidx,
              page_table, w_in, kv_norm, kidx_norm, w_uk, w_uv, w_o_a, w_o_b, w_out, rel_bias,
              norm_attn, norm_ffn, norm_final, w_ada, b_ada, w_grp, b_grp, w_er, b_er, w1, w3, w2):
    past = page_table.shape[1] * PAGE_SIZE
    pos_p = jnp.arange(x_prompt.shape[1], dtype=jnp.int32)
    pos_s = past + jnp.arange(x_sample.shape[1], dtype=jnp.int32)
    xp, xs = x_prompt, x_sample
    rows_p, rows_s = [], []
    for l in range(DEPTH):
        layer_w = (w_in[l], kv_norm[l], kidx_norm[l], w_o_a[l], w_o_b[l], w_out[l], norm_attn[l], norm_ffn[l],
                   w_ada[l], b_ada[l], w_grp[l], b_grp[l], w_er[l], b_er[l], w1[l], w3[l], w2[l])
        mix_p = functools.partial(prompt_mix, w_uk[l], w_uv[l], rel_bias)
        mix_s = functools.partial(sample_mix, cache_ckv, cache_krope, cache_k, cache_v, cache_kidx, page_table, l,
                                  w_uk[l], w_uv[l], rel_bias)
        xp, r_p = trunk_layer(xp, c_prompt, pos_p, mix_p, *layer_w)
        xs, r_s = trunk_layer(xs, c_sample, pos_s, mix_s, *layer_w)
        rows_p.append(r_p)
        rows_s.append(r_s)
    y_prompt = rmsnorm(xp, norm_final)
    y_sample = rmsnorm(xs, norm_final)
    new_ckv_p = jnp.stack([r[0] for r in rows_p])
    new_krope_p = jnp.stack([r[1] for r in rows_p])
    new_k_p = jnp.stack([r[2] for r in rows_p])
    new_v_p = jnp.stack([r[3] for r in rows_p])
    new_kidx_p = jnp.stack([r[4] for r in rows_p])
    new_ckv_s = jnp.stack([r[0] for r in rows_s])
    new_krope_s = jnp.stack([r[1] for r in rows_s])
    new_k_s = jnp.stack([r[2] for r in rows_s])
    new_v_s = jnp.stack([r[3] for r in rows_s])
    new_kidx_s = jnp.stack([r[4] for r in rows_s])
    return (y_prompt, y_sample, new_ckv_p, new_krope_p, new_k_p, new_v_p, new_kidx_p,
            new_ckv_s, new_krope_s, new_k_s, new_v_s, new_kidx_s)
```

```python
import functools
import math

import jax
import jax.numpy as jnp
import numpy as np
from jax import lax
from jax.experimental import pallas as pl
from jax.experimental.pallas import tpu as pltpu

F32 = jnp.float32
BF = jnp.bfloat16
I32 = jnp.int32

D_MODEL = 2048
PAGE = 128
A_HEADS, A_NOPE, A_ROPE, A_V, KV_LORA = 8, 128, 64, 128, 512
ROPE_THETA = 10000.0
B_HEADS, B_KV_HEADS, B_HEAD_DIM = 8, 2, 128
B_REP = B_HEADS // B_KV_HEADS
IDX_HEADS, IDX_DIM, TOPK_MAX = 16, 64, 256
REL_BUCKETS, REL_MAX_DIST = 32, 128
N_GROUPS, EPG, TOP_K_EXPERTS, D_EXPERT = 4, 8, 2, 512
N_EXPERTS = N_GROUPS * EPG
EPS = 1e-6

NEG_BIG = -1e30
INT_MIN = -(2 ** 31)
KEY_NEG_INF = int(np.array(np.float32(-np.inf)).view(np.int32) ^ np.int32(0x7FFFFFFF))

NT_DIMS = (((1,), (1,)), ((), ()))


def _params(vmem_mb, sem):
    return pltpu.CompilerParams(vmem_limit_bytes=vmem_mb * 1024 * 1024, dimension_semantics=sem)


def _dot(a, b):
    return jnp.dot(a, b, preferred_element_type=F32)


def _dot_nt(a, b):
    return lax.dot_general(a, b, NT_DIMS, preferred_element_type=F32)


def _float_key(x):
    bits = lax.bitcast_convert_type(x, I32)
    return bits ^ ((bits >> 31) & jnp.int32(0x7FFFFFFF))


def _adaln_kernel(c_ref, w_ref, b_ref, o_ref):
    c = c_ref[...]
    s = (c * jax.nn.sigmoid(c)).astype(BF)
    o_ref[...] = _dot(s, w_ref[...].astype(BF)) + b_ref[...]


def _adaln(c_all, w_ada, b_ada):
    mc, d = c_all.shape
    n = w_ada.shape[1]
    tn = 1024
    return pl.pallas_call(
        _adaln_kernel, grid=(n // tn,),
        in_specs=[pl.BlockSpec((mc, d), lambda j: (0, 0)),
                  pl.BlockSpec((d, tn), lambda j: (0, j)),
                  pl.BlockSpec((1, tn), lambda j: (0, j))],
        out_specs=pl.BlockSpec((mc, tn), lambda j: (0, j)),
        out_shape=jax.ShapeDtypeStruct((mc, n), F32),
        compiler_params=_params(48, ("arbitrary",)),
    )(c_all, w_ada, b_ada.reshape(1, n))


class _Group:
    def __init__(self, m, tm, mod, mod_spec, cos1, sin1, cos8, sin8, act_dtype):
        self.m, self.tm = m, tm
        self.mod, self.mod_spec = mod, mod_spec
        self.cos1, self.sin1, self.cos8, self.sin8 = cos1, sin1, cos8, sin8
        self.act_dtype = act_dtype

    def tab_spec(self, tab):
        nt = tab.shape[0] // self.tm
        return pl.BlockSpec((self.tm, tab.shape[1]), lambda j, i: (i % nt, 0))


def _rope_tables(pos):
    half = A_ROPE // 2
    inv = ROPE_THETA ** (-jnp.arange(half, dtype=F32) / half)
    ang = pos.astype(F32)[:, None] * inv[None, :]
    cos, sin = jnp.cos(ang), jnp.sin(ang)
    return jnp.concatenate([cos, cos], -1), jnp.concatenate([-sin, sin], -1)


def _modulate_kernel(x_ref, g_ref, sc_ref, sh_ref, o_ref):
    x = x_ref[...]
    r = lax.rsqrt(jnp.mean(x * x, axis=-1, keepdims=True) + EPS)
    y = x * r * g_ref[...]
    o_ref[...] = (y * (1.0 + sc_ref[...]) + sh_ref[...]).astype(o_ref.dtype)


def _modulate(g, x, gain, k_scale, k_shift):
    d = x.shape[1]
    row = pl.BlockSpec((g.tm, d), lambda j, i: (i, 0))
    return pl.pallas_call(
        _modulate_kernel, grid=(1, g.m // g.tm),
        in_specs=[row, pl.BlockSpec((1, d), lambda j, i: (0, 0)), g.mod_spec(k_scale), g.mod_spec(k_shift)],
        out_specs=row, out_shape=jax.ShapeDtypeStruct((g.m, d), BF),
        compiler_params=_params(48, ("arbitrary", "arbitrary")),
    )(x, gain.reshape(1, d), g.mod, g.mod)


def _proj_scale_kernel(x_ref, w_ref, s_ref, o_ref):
    o_ref[...] = (_dot(x_ref[...], w_ref[...]) * s_ref[...]).astype(o_ref.dtype)


def _proj_scale(m, tm, x, w, scale, out_dtype, tn=512):
    k, n = w.shape
    return pl.pallas_call(
        _proj_scale_kernel, grid=(n // tn, m // tm),
        in_specs=[pl.BlockSpec((tm, k), lambda j, i: (i, 0)),
                  pl.BlockSpec((k, tn), lambda j, i: (0, j)),
                  pl.BlockSpec((1, tn), lambda j, i: (0, j))],
        out_specs=pl.BlockSpec((tm, tn), lambda j, i: (i, j)),
        out_shape=jax.ShapeDtypeStruct((m, n), out_dtype),
        compiler_params=_params(48, ("arbitrary", "arbitrary")),
    )(x, w, scale.reshape(1, n))


def _proj_rope_kernel(x_ref, w_ref, wr_ref, cos_ref, sin_ref, o_ref, *, scale):
    x = x_ref[...]
    r = (_dot(x, w_ref[...]) * cos_ref[...] + _dot(x, wr_ref[...]) * sin_ref[...]) * scale
    for h in range(A_HEADS):
        o_ref[h] = r[:, h * A_ROPE:(h + 1) * A_ROPE].astype(o_ref.dtype)


def _proj_rope(g, x, w, wr, scale):
    k, n = w.shape
    return pl.pallas_call(
        functools.partial(_proj_rope_kernel, scale=scale), grid=(1, g.m // g.tm),
        in_specs=[pl.BlockSpec((g.tm, k), lambda j, i: (i, 0)),
                  pl.BlockSpec((k, n), lambda j, i: (0, 0)),
                  pl.BlockSpec((k, n), lambda j, i: (0, 0)),
                  g.tab_spec(g.cos8), g.tab_spec(g.sin8)],
        out_specs=pl.BlockSpec((A_HEADS, g.tm, A_ROPE), lambda j, i: (0, i, 0)),
        out_shape=jax.ShapeDtypeStruct((A_HEADS, g.m, A_ROPE), g.act_dtype),
        compiler_params=_params(48, ("arbitrary", "arbitrary")),
    )(x, w, wr, g.cos8, g.sin8)


_SM_CKV, _SM_KR, _SM_KRR = 0, KV_LORA, KV_LORA + A_ROPE
_SM_KB = KV_LORA + 2 * A_ROPE
_KVB = B_KV_HEADS * B_HEAD_DIM
_SM_VB = _SM_KB + _KVB
_SM_KI = _SM_VB + _KVB
_SM_WI = _SM_KI + IDX_DIM
_SM_COLS = 1280


def _proj_small_kernel(x_ref, w_ref, cos_ref, sin_ref, kvn_ref, kin_ref,
                       ckv_ref, kr_ref, kb_ref, vb_ref, ki_ref, wi_ref,
                       ckvb_ref, krb_ref, kbb_ref, vbb_ref, kib_ref):
    a = _dot(x_ref[...], w_ref[...])
    c = a[:, _SM_CKV:_SM_CKV + KV_LORA]
    ckv = c * lax.rsqrt(jnp.mean(c * c, axis=-1, keepdims=True) + EPS) * kvn_ref[...]
    kr = (a[:, _SM_KR:_SM_KR + A_ROPE] * cos_ref[...] + a[:, _SM_KRR:_SM_KRR + A_ROPE] * sin_ref[...])
    kb = a[:, _SM_KB:_SM_KB + _KVB]
    vb = a[:, _SM_VB:_SM_VB + _KVB]
    k = a[:, _SM_KI:_SM_KI + IDX_DIM]
    ki = k * lax.rsqrt(jnp.mean(k * k, axis=-1, keepdims=True) + EPS) * kin_ref[...]
    wi = a[:, _SM_WI:_SM_WI + IDX_HEADS] * (IDX_HEADS ** -0.5 * IDX_DIM ** -0.5)
    ckv_ref[...] = ckv
    kr_ref[...] = kr
    kb_ref[...] = kb
    vb_ref[...] = vb
    ki_ref[...] = ki
    wi_ref[...] = wi
    ckvb_ref[...] = ckv.astype(BF)
    krb_ref[...] = kr.astype(BF)
    kbb_ref[...] = kb.astype(BF)
    vbb_ref[...] = vb.astype(BF)
    kib_ref[...] = ki.astype(BF)


def _proj_small(g, x, w, kv_norm, kidx_norm):
    k = x.shape[1]
    widths = (KV_LORA, A_ROPE, _KVB, _KVB, IDX_DIM, IDX_HEADS, KV_LORA, A_ROPE, _KVB, _KVB, IDX_DIM)
    dtypes = (F32,) * 6 + (BF,) * 5
    return pl.pallas_call(
        _proj_small_kernel, grid=(1, g.m // g.tm),
        in_specs=[pl.BlockSpec((g.tm, k), lambda j, i: (i, 0)),
                  pl.BlockSpec((k, _SM_COLS), lambda j, i: (0, 0)),
                  g.tab_spec(g.cos1), g.tab_spec(g.sin1),
                  pl.BlockSpec((1, KV_LORA), lambda j, i: (0, 0)),
                  pl.BlockSpec((1, IDX_DIM), lambda j, i: (0, 0))],
        out_specs=[pl.BlockSpec((g.tm, wd), lambda j, i: (i, 0)) for wd in widths],
        out_shape=[jax.ShapeDtypeStruct((g.m, wd), dt) for wd, dt in zip(widths, dtypes)],
        compiler_params=_params(48, ("arbitrary", "arbitrary")),
    )(x, w, g.cos1, g.sin1, kv_norm.reshape(1, -1), kidx_norm.reshape(1, -1))


def _mla_prompt_kernel(qn_ref, qr_ref, kn_ref, kr_ref, v_ref, o_ref, *, t):
    qi = pl.program_id(2)
    qn = qn_ref[...]
    qr = qr_ref[...]

    def chunk(c, carry, diag):
        m, l, acc = carry
        off = pl.multiple_of(c * t, t)
        s = _dot_nt(qn, kn_ref[pl.ds(off, t), :]) + _dot_nt(qr, kr_ref[pl.ds(off, t), :])
        if diag:
            row = lax.broadcasted_iota(I32, (t, t), 0)
            col = lax.broadcasted_iota(I32, (t, t), 1)
            s = jnp.where(col <= row, s, NEG_BIG)
        m_new = jnp.maximum(m, jnp.max(s, axis=-1, keepdims=True))
        p = jnp.exp(s - m_new)
        corr = jnp.exp(m - m_new)
        l = l * corr + jnp.sum(p, axis=-1, keepdims=True)
        acc = acc * corr + _dot(p.astype(BF), v_ref[pl.ds(off, t), :])
        return m_new, l, acc

    init = (jnp.full((t, 1), NEG_BIG, F32), jnp.zeros((t, 1), F32), jnp.zeros((t, A_V), F32))
    carry = chunk(qi, init, True)
    m, l, acc = lax.fori_loop(0, qi, lambda c, cr: chunk(c, cr, False), carry)
    o_ref[...] = (acc / l).astype(o_ref.dtype)


def _mla_prompt(nb, s, q_all, q_rope, kv_up, kr_bf, t):
    nq = s // t
    return pl.pallas_call(
        functools.partial(_mla_prompt_kernel, t=t), grid=(nb, A_HEADS, nq),
        in_specs=[pl.BlockSpec((t, A_NOPE), lambda b, h, i: (b * nq + i, h)),
                  pl.BlockSpec((None, t, A_ROPE), lambda b, h, i: (h, b * nq + i, 0)),
                  pl.BlockSpec((s, A_NOPE), lambda b, h, i: (b, h)),
                  pl.BlockSpec((s, A_ROPE), lambda b, h, i: (b, 0)),
                  pl.BlockSpec((s, A_V), lambda b, h, i: (b, A_HEADS + h))],
        out_specs=pl.BlockSpec((t, A_V), lambda b, h, i: (b * nq + i, h)),
        out_shape=jax.ShapeDtypeStruct((nb * s, A_HEADS * A_V), BF),
        compiler_params=_params(48, ("arbitrary", "arbitrary", "arbitrary")),
    )(q_all, q_rope, kv_up, kr_bf, kv_up)


def _kth_largest_key(count_ge, n_rows, total, k):
    def body(b, carry):
        lo, cnt_lo = carry
        cand = lo + lax.shift_left(jnp.int32(1), jnp.int32(31) - b)
        cnt = count_ge(cand)
        ok = cnt >= float(k)
        return jnp.where(ok, cand, lo), jnp.where(ok, cnt, cnt_lo)

    init = (jnp.full((n_rows, 1), INT_MIN, I32), jnp.full((n_rows, 1), float(total), F32))
    return lax.fori_loop(0, 32, body, init)


def _tie_cutoff(count_eq_below, n_rows, need, n_bits):
    def body(b, tcur):
        cand = tcur + lax.shift_left(jnp.int32(1), jnp.int32(n_bits - 1) - b)
        return jnp.where(count_eq_below(cand) <= need, cand, tcur)

    return lax.fori_loop(0, n_bits, body, jnp.zeros((n_rows, 1), I32))


def _fold_lanes(x):
    acc = x[:, :128]
    for j in range(1, x.shape[1] // 128):
        acc = acc + x[:, j * 128:(j + 1) * 128]
    return acc


def _dsa_prompt_kernel(cfar_ref, qi_ref, wi_ref, ki_ref, qb_ref, kb_ref, vb_ref, bd_ref, bp_ref,
                       o_ref, key_ref, *, t, n_sel):
    qi = pl.program_id(1)
    n_valid = qi + 1
    row = lax.broadcasted_iota(I32, (t, t), 0)
    col = lax.broadcasted_iota(I32, (t, t), 1)
    causal = col <= row

    q_i = qi_ref[...]
    w_i = wi_ref[...]
    q_heads = [q_i[:, h * IDX_DIM:(h + 1) * IDX_DIM] for h in range(IDX_HEADS)]
    w_heads = [w_i[:, h:h + 1] for h in range(IDX_HEADS)]

    def score_chunk(c, diag):
        kc = ki_ref[pl.ds(pl.multiple_of(c * t, t), t), :]
        acc = jnp.zeros((t, t), F32)
        for h in range(IDX_HEADS):
            acc = acc + jnp.maximum(_dot_nt(q_heads[h], kc), 0.0) * w_heads[h]
        if diag:
            acc = jnp.where(causal, acc, -jnp.inf)
        key_ref[c] = _float_key(acc)

    score_chunk(qi, True)

    def score_body(c, carry):
        score_chunk(c, False)
        return carry

    lax.fori_loop(0, qi, score_body, 0)

    def count_where(pred):
        def body(c, acc):
            return acc + _fold_lanes(jnp.where(pred(key_ref[c], c), 1.0, 0.0))
        part = lax.fori_loop(0, n_valid, body, jnp.zeros((t, 128), F32))
        return jnp.sum(part, axis=-1, keepdims=True)

    thr, cnt_ge = _kth_largest_key(lambda cand: count_where(lambda kc, c: kc >= cand),
                                   t, 0, n_sel)
    tie = (cnt_ge > float(n_sel)) & (thr > KEY_NEG_INF)

    @pl.when(jnp.max(jnp.where(tie, 1.0, 0.0)) > 0.0)
    def _():
        cnt_gt = count_where(lambda kc, c: kc > thr)
        need = jnp.where(tie, float(n_sel) - cnt_gt, 1e9)
        n_bits = int(math.ceil(math.log2(key_ref.shape[0] * t))) + 1
        tstar = _tie_cutoff(
            lambda cand: count_where(lambda kc, c: (kc == thr) & (col + c * t < cand)),
            t, need, n_bits)

        def demote(c, carry):
            kc = key_ref[c]
            key_ref[c] = jnp.where((kc == thr) & (col + c * t >= tstar) & tie, thr - 1, kc)
            return carry

        lax.fori_loop(0, n_valid, demote, 0)

    q_b = qb_ref[...]
    q_groups = [jnp.concatenate([q_b[:, (g * B_REP + r) * B_HEAD_DIM:(g * B_REP + r + 1) * B_HEAD_DIM]
                                 for r in range(B_REP)], axis=0) for g in range(B_KV_HEADS)]

    def attend(c, carry, kind):
        off = pl.multiple_of(c * t, t)
        sel = key_ref[c] >= thr
        if kind == "diag":
            sel = sel & causal
        kc = kb_ref[pl.ds(off, t), :]
        vc = vb_ref[pl.ds(off, t), :]
        new = []
        for g in range(B_KV_HEADS):
            m, l, acc = carry[g]
            s_all = _dot_nt(q_groups[g], kc[:, g * B_HEAD_DIM:(g + 1) * B_HEAD_DIM])
            parts = []
            for r in range(B_REP):
                h = g * B_REP + r
                s = s_all[r * t:(r + 1) * t]
                if kind == "far":
                    s = s + cfar_ref[h]
                elif kind == "prev":
                    s = s + bp_ref[h]
                else:
                    s = s + bd_ref[h]
                parts.append(jnp.where(sel, s, NEG_BIG))
            s_all = jnp.concatenate(parts, axis=0)
            m_new = jnp.maximum(m, jnp.max(s_all, axis=-1, keepdims=True))
            p = jnp.exp(s_all - m_new)
            corr = jnp.exp(m - m_new)
            l = l * corr + jnp.sum(p, axis=-1, keepdims=True)
            acc = acc * corr + _dot(p.astype(BF), vc[:, g * B_HEAD_DIM:(g + 1) * B_HEAD_DIM])
            new.append((m_new, l, acc))
        return tuple(new)

    rows = B_REP * t
    init = tuple((jnp.full((rows, 1), NEG_BIG, F32), jnp.zeros((rows, 1), F32),
                  jnp.zeros((rows, B_HEAD_DIM), F32)) for _ in range(B_KV_HEADS))
    carry = attend(qi, init, "diag")
    carry = lax.fori_loop(jnp.maximum(qi - 1, 0), qi, lambda c, cr: attend(c, cr, "prev"), carry)
    carry = lax.fori_loop(0, jnp.maximum(qi - 1, 0), lambda c, cr: attend(c, cr, "far"), carry)
    for g in range(B_KV_HEADS):
        m, l, acc = carry[g]
        o = acc / l
        for r in range(B_REP):
            h = g * B_REP + r
            o_ref[:, h * B_HEAD_DIM:(h + 1) * B_HEAD_DIM] = o[r * t:(r + 1) * t].astype(o_ref.dtype)


def _t5_bucket(dist):
    n = jnp.maximum(dist, 0)
    max_exact = REL_BUCKETS // 2
    nf = jnp.maximum(n, 1).astype(F32)
    large = max_exact + (jnp.log(nf / max_exact) / math.log(REL_MAX_DIST / max_exact)
                         * (REL_BUCKETS - max_exact)).astype(I32)
    large = jnp.minimum(large, REL_BUCKETS - 1)
    return jnp.where(n < max_exact, n, large)


def _bias_table(rel_bias, dist):
    return jnp.moveaxis(rel_bias[_t5_bucket(dist)], -1, 0).astype(F32)


def _dsa_prompt(nb, s, q_all, wi, ki_bf, kb_bf, vb_bf, rel_bias, t):
    assert t + 1 >= REL_MAX_DIST
    nq = s // t
    n_sel = min(TOPK_MAX, s // 4)
    ii = jnp.arange(t)[:, None] - jnp.arange(t)[None, :]
    b_diag = _bias_table(rel_bias, ii)
    b_prev = _bias_table(rel_bias, ii + t)
    c_far = rel_bias[REL_BUCKETS - 1].astype(F32)
    qcols = (A_HEADS * A_NOPE) // (B_HEADS * B_HEAD_DIM)
    return pl.pallas_call(
        functools.partial(_dsa_prompt_kernel, t=t, n_sel=n_sel), grid=(nb, nq),
        in_specs=[pl.BlockSpec(memory_space=pltpu.SMEM),
                  pl.BlockSpec((t, IDX_HEADS * IDX_DIM), lambda b, i: (b * nq + i, qcols + 1)),
                  pl.BlockSpec((t, IDX_HEADS), lambda b, i: (b * nq + i, 0)),
                  pl.BlockSpec((s, IDX_DIM), lambda b, i: (b, 0)),
                  pl.BlockSpec((t, B_HEADS * B_HEAD_DIM), lambda b, i: (b * nq + i, qcols)),
                  pl.BlockSpec((s, _KVB), lambda b, i: (b, 0)),
                  pl.BlockSpec((s, _KVB), lambda b, i: (b, 0)),
                  pl.BlockSpec((B_HEADS, t, t), lambda b, i: (0, 0, 0)),
                  pl.BlockSpec((B_HEADS, t, t), lambda b, i: (0, 0, 0))],
        out_specs=pl.BlockSpec((t, B_HEADS * B_HEAD_DIM), lambda b, i: (b * nq + i, 0)),
        out_shape=jax.ShapeDtypeStruct((nb * s, B_HEADS * B_HEAD_DIM), BF),
        scratch_shapes=[pltpu.VMEM((nq, t, t), I32)],
        compiler_params=_params(56, ("arbitrary", "arbitrary")),
    )(c_far, q_all, wi, ki_bf, q_all, kb_bf, vb_bf, b_diag, b_prev)


def _page_copies(pt_ref, seq, chunk, slot, n_pages, g_pages, srcs, bufs, sems):
    copies = []
    for p in range(g_pages):
        page = pt_ref[seq * n_pages + chunk * g_pages + p]
        for k, (src, buf) in enumerate(zip(srcs, bufs)):
            copies.append(pltpu.make_async_copy(src.at[page], buf.at[slot, pl.ds(p * PAGE, PAGE)], sems.at[k, slot]))
    return copies


def _paged_loop(pt_ref, n_seq, n_pages, g_pages, srcs, bufs, sems, body, carry):
    b = pl.program_id(0)
    n_chunks = n_pages // g_pages
    copies = functools.partial(_page_copies, pt_ref, n_pages=n_pages, g_pages=g_pages,
                               srcs=srcs, bufs=bufs, sems=sems)

    @pl.when(b == 0)
    def _():
        for cp in copies(0, 0, 0):
            cp.start()

    for c in range(n_chunks):
        gidx = b * n_chunks + c
        slot = gidx % 2
        if c + 1 < n_chunks:
            for cp in copies(b, c + 1, 1 - slot):
                cp.start()
        else:
            @pl.when(b + 1 < n_seq)
            def _():
                for cp in copies(b + 1, 0, 1 - slot):
                    cp.start()
        for cp in copies(b, c, slot):
            cp.wait()
        carry = body(c, slot, carry)
    return carry


def _softmax_step(carry, s, v_bf):
    m, l, acc = carry
    m_new = jnp.maximum(m, jnp.max(s, axis=-1, keepdims=True))
    p = jnp.exp(s - m_new)
    corr = jnp.exp(m - m_new)
    l = l * corr + jnp.sum(p, axis=-1, keepdims=True)
    acc = acc * corr + _dot(p.astype(BF), v_bf)
    return m_new, l, acc


def _pad_rows(x, n):
    return jnp.concatenate([x, jnp.zeros((n - x.shape[0], x.shape[1]), x.dtype)], axis=0)


def _mla_sample_kernel(pt_ref, ql_ref, qr_ref, cn_ref, kn_ref, ck_hbm, kr_hbm, o_ref,
                       ck_buf, kr_buf, sems, *, n_seq, n_pages, g_pages, t_new):
    rows = A_HEADS * t_new
    ql = ql_ref[...].reshape(rows, KV_LORA).astype(BF)
    qr = qr_ref[...].reshape(rows, A_ROPE).astype(BF)

    cn = _pad_rows(cn_ref[...], PAGE).astype(BF)
    kn = _pad_rows(kn_ref[...], PAGE).astype(BF)
    s = _dot_nt(ql, cn) + _dot_nt(qr, kn)
    tok = lax.broadcasted_iota(I32, (rows, PAGE), 0) % t_new
    col = lax.broadcasted_iota(I32, (rows, PAGE), 1)
    s = jnp.where(col <= tok, s, NEG_BIG)
    init = (jnp.full((rows, 1), NEG_BIG, F32), jnp.zeros((rows, 1), F32), jnp.zeros((rows, KV_LORA), F32))
    carry = _softmax_step(init, s, cn)

    def body(c, slot, carry):
        ck = ck_buf[slot].astype(BF)
        kr = kr_buf[slot].astype(BF)
        return _softmax_step(carry, _dot_nt(ql, ck) + _dot_nt(qr, kr), ck)

    m, l, acc = _paged_loop(pt_ref, n_seq, n_pages, g_pages, (ck_hbm, kr_hbm), (ck_buf, kr_buf), sems, body, carry)
    o_ref[...] = (acc / l).reshape(A_HEADS, t_new, KV_LORA)


def _mla_sample(pt_flat, q_lat, q_rope, ckv_new, kr_new, cache_ckv, cache_krope, n_seq, t_new, n_pages, g_pages):
    grid_spec = pltpu.PrefetchScalarGridSpec(
        num_scalar_prefetch=1, grid=(n_seq,),
        in_specs=[pl.BlockSpec((A_HEADS, t_new, KV_LORA), lambda b, pt: (0, b, 0)),
                  pl.BlockSpec((A_HEADS, t_new, A_ROPE), lambda b, pt: (0, b, 0)),
                  pl.BlockSpec((t_new, KV_LORA), lambda b, pt: (b, 0)),
                  pl.BlockSpec((t_new, A_ROPE), lambda b, pt: (b, 0)),
                  pl.BlockSpec(memory_space=pl.ANY),
                  pl.BlockSpec(memory_space=pl.ANY)],
        out_specs=pl.BlockSpec((A_HEADS, t_new, KV_LORA), lambda b, pt: (0, b, 0)),
        scratch_shapes=[pltpu.VMEM((2, g_pages * PAGE, KV_LORA), F32),
                        pltpu.VMEM((2, g_pages * PAGE, A_ROPE), F32),
                        pltpu.SemaphoreType.DMA((2, 2))])
    return pl.pallas_call(
        functools.partial(_mla_sample_kernel, n_seq=n_seq, n_pages=n_pages, g_pages=g_pages, t_new=t_new),
        grid_spec=grid_spec,
        out_shape=jax.ShapeDtypeStruct((A_HEADS, n_seq * t_new, KV_LORA), F32),
        compiler_params=_params(56, ("arbitrary",)),
    )(pt_flat, q_lat, q_rope, ckv_new, kr_new, cache_ckv, cache_krope)


def _headmm_kernel(x_ref, w_ref, o_ref):
    o_ref[...] = _dot(x_ref[...].astype(BF), w_ref[...]).astype(o_ref.dtype)


def _head_matmul(x, x_spec, w, out_shape, out_spec):
    nh, k, n = w.shape
    return pl.pallas_call(
        _headmm_kernel, grid=(nh,),
        in_specs=[x_spec, pl.BlockSpec((None, k, n), lambda h: (h, 0, 0))],
        out_specs=out_spec, out_shape=out_shape,
        compiler_params=_params(48, ("arbitrary",)),
    )(x, w)


def _dsa_sample_index_kernel(pt_ref, qi_ref, wi_ref, kin_ref, kidx_hbm, key_out, thr_out,
                             kidx_buf, sems, key_ref, keyn_ref, *, n_seq, n_pages, g_pages, t_new, n_sel):
    n_chunks = n_pages // g_pages
    width = g_pages * PAGE
    past = n_pages * PAGE
    q_i = qi_ref[...].astype(BF)
    w_i = wi_ref[...]
    q_st = jnp.concatenate([q_i[:, h * IDX_DIM:(h + 1) * IDX_DIM] for h in range(IDX_HEADS)], axis=0)
    w_st = jnp.concatenate([w_i[:, h:h + 1] for h in range(IDX_HEADS)], axis=0)

    def scores(k_bf):
        r = jnp.maximum(_dot_nt(q_st, k_bf), 0.0) * w_st
        return jnp.sum(r.reshape(IDX_HEADS, t_new, k_bf.shape[0]), axis=0)

    s_new = scores(_pad_rows(kin_ref[...], PAGE).astype(BF))
    tok = lax.broadcasted_iota(I32, (t_new, PAGE), 0)
    coln = lax.broadcasted_iota(I32, (t_new, PAGE), 1)
    keyn_ref[...] = _float_key(jnp.where(coln <= tok, s_new, -jnp.inf))

    def body(c, slot, carry):
        key_ref[c] = _float_key(scores(kidx_buf[slot].astype(BF)))
        return carry

    _paged_loop(pt_ref, n_seq, n_pages, g_pages, (kidx_hbm,), (kidx_buf,), sems, body, 0)

    colw = lax.broadcasted_iota(I32, (t_new, width), 1)

    def count_where(pred):
        part = _fold_lanes(jnp.where(pred(keyn_ref[...], coln + past), 1.0, 0.0))
        for c in range(n_chunks):
            part = part + _fold_lanes(jnp.where(pred(key_ref[c], colw + c * width), 1.0, 0.0))
        return jnp.sum(part, axis=-1, keepdims=True)

    thr, cnt_ge = _kth_largest_key(lambda cand: count_where(lambda kc, ci: kc >= cand), t_new, 0, n_sel)
    tie = (cnt_ge > float(n_sel)) & (thr > KEY_NEG_INF)

    @pl.when(jnp.max(jnp.where(tie, 1.0, 0.0)) > 0.0)
    def _():
        cnt_gt = count_where(lambda kc, ci: kc > thr)
        need = jnp.where(tie, float(n_sel) - cnt_gt, 1e9)
        n_bits = int(math.ceil(math.log2(past + PAGE))) + 1
        tstar = _tie_cutoff(lambda cand: count_where(lambda kc, ci: (kc == thr) & (ci < cand)),
                            t_new, need, n_bits)
        for c in range(n_chunks):
            kc = key_ref[c]
            key_ref[c] = jnp.where((kc == thr) & (colw + c * width >= tstar) & tie, thr - 1, kc)
        kc = keyn_ref[...]
        keyn_ref[...] = jnp.where((kc == thr) & (coln + past >= tstar) & tie, thr - 1, kc)

    for c in range(n_chunks):
        key_out[:, c * width:(c + 1) * width] = key_ref[c]
    key_out[:, past:past + PAGE] = keyn_ref[...]
    thr_out[...] = jnp.broadcast_to(thr, (t_new, PAGE))


def _dsa_sample_index(pt_flat, q_i, w_i, ki_new, cache_kidx, n_seq, t_new, n_pages, g_pages, n_sel):
    past = n_pages * PAGE
    grid_spec = pltpu.PrefetchScalarGridSpec(
        num_scalar_prefetch=1, grid=(n_seq,),
        in_specs=[pl.BlockSpec((t_new, IDX_HEADS * IDX_DIM), lambda b, pt: (b, 0)),
                  pl.BlockSpec((t_new, IDX_HEADS), lambda b, pt: (b, 0)),
                  pl.BlockSpec((t_new, IDX_DIM), lambda b, pt: (b, 0)),
                  pl.BlockSpec(memory_space=pl.ANY)],
        out_specs=[pl.BlockSpec((None, t_new, past + PAGE), lambda b, pt: (b, 0, 0)),
                   pl.BlockSpec((None, t_new, PAGE), lambda b, pt: (b, 0, 0))],
        scratch_shapes=[pltpu.VMEM((2, g_pages * PAGE, IDX_DIM), F32),
                        pltpu.SemaphoreType.DMA((1, 2)),
                        pltpu.VMEM((n_pages // g_pages, t_new, g_pages * PAGE), I32),
                        pltpu.VMEM((t_new, PAGE), I32)])
    return pl.pallas_call(
        functools.partial(_dsa_sample_index_kernel, n_seq=n_seq, n_pages=n_pages, g_pages=g_pages,
                          t_new=t_new, n_sel=n_sel),
        grid_spec=grid_spec,
        out_shape=[jax.ShapeDtypeStruct((n_seq, t_new, past + PAGE), I32),
                   jax.ShapeDtypeStruct((n_seq, t_new, PAGE), I32)],
        compiler_params=_params(48, ("arbitrary",)),
    )(pt_flat, q_i, w_i, ki_new, cache_kidx)


def _dsa_sample_attn_kernel(pt_ref, cfar_ref, qb_ref, kn_ref, vn_ref, key_ref, thr_ref, blast_ref, bnew_ref,
                            k_hbm, v_hbm, o_ref, k_buf, v_buf, sems, *, n_seq, n_pages, g_pages, t_new):
    n_chunks = n_pages // g_pages
    width = g_pages * PAGE
    past = n_pages * PAGE
    rows = B_REP * t_new
    q_b = qb_ref[...].astype(BF)
    thr = thr_ref[:, :1]
    q_groups = [jnp.concatenate([q_b[:, (g * B_REP + r) * B_HEAD_DIM:(g * B_REP + r + 1) * B_HEAD_DIM]
                                 for r in range(B_REP)], axis=0) for g in range(B_KV_HEADS)]
    far_cols = [jnp.concatenate([jnp.full((t_new, 1), cfar_ref[g * B_REP + r], F32) for r in range(B_REP)], axis=0)
                for g in range(B_KV_HEADS)]

    def step(carry, k_bf, v_bf, sel, bias_fn):
        sel4 = jnp.concatenate([sel] * B_REP, axis=0)
        new = []
        for g in range(B_KV_HEADS):
            s = _dot_nt(q_groups[g], k_bf[:, g * B_HEAD_DIM:(g + 1) * B_HEAD_DIM]) + bias_fn(g)
            new.append(_softmax_step(carry[g], jnp.where(sel4, s, NEG_BIG), v_bf[:, g * B_HEAD_DIM:(g + 1) * B_HEAD_DIM]))
        return tuple(new)

    tok = lax.broadcasted_iota(I32, (t_new, PAGE), 0)
    coln = lax.broadcasted_iota(I32, (t_new, PAGE), 1)
    sel_new = (key_ref[:, past:past + PAGE] >= thr) & (coln <= tok)
    init = tuple((jnp.full((rows, 1), NEG_BIG, F32), jnp.zeros((rows, 1), F32), jnp.zeros((rows, B_HEAD_DIM), F32))
                 for _ in range(B_KV_HEADS))
    carry = step(init, _pad_rows(kn_ref[...], PAGE).astype(BF), _pad_rows(vn_ref[...], PAGE).astype(BF), sel_new,
                 lambda g: bnew_ref[g * rows:(g + 1) * rows, :])

    def body(c, slot, carry):
        sel = key_ref[:, c * width:(c + 1) * width] >= thr
        if c == n_chunks - 1:
            bias_fn = lambda g: blast_ref[g * rows:(g + 1) * rows, :]
        else:
            bias_fn = lambda g: far_cols[g]
        return step(carry, k_buf[slot].astype(BF), v_buf[slot].astype(BF), sel, bias_fn)

    carry = _paged_loop(pt_ref, n_seq, n_pages, g_pages, (k_hbm, v_hbm), (k_buf, v_buf), sems, body, carry)
    for g in range(B_KV_HEADS):
        m, l, acc = carry[g]
        o = acc / l
        for r in range(B_REP):
            h = g * B_REP + r
            o_ref[:, h * B_HEAD_DIM:(h + 1) * B_HEAD_DIM] = o[r * t_new:(r + 1) * t_new].astype(o_ref.dtype)


def _dsa_sample_attn(pt_flat, q_all, qcol, k_new, v_new, keys, thr, rel_bias, cache_k, cache_v,
                     n_seq, t_new, n_pages, g_pages):
    past = n_pages * PAGE
    width = g_pages * PAGE
    assert past - (n_pages - 1) * PAGE + 1 >= REL_MAX_DIST
    tq = jnp.arange(t_new)
    d_last = (past + tq)[:, None] - (past - width + jnp.arange(width))[None, :]
    b_last = _bias_table(rel_bias, d_last).reshape(B_HEADS * t_new, width)
    d_new = tq[:, None] - jnp.arange(PAGE)[None, :]
    b_new = _bias_table(rel_bias, d_new).reshape(B_HEADS * t_new, PAGE)
    c_far = rel_bias[REL_BUCKETS - 1].astype(F32)
    grid_spec = pltpu.PrefetchScalarGridSpec(
        num_scalar_prefetch=1, grid=(n_seq,),
        in_specs=[pl.BlockSpec(memory_space=pltpu.SMEM),
                  pl.BlockSpec((t_new, B_HEADS * B_HEAD_DIM), lambda b, pt: (b, qcol)),
                  pl.BlockSpec((t_new, _KVB), lambda b, pt: (b, 0)),
                  pl.BlockSpec((t_new, _KVB), lambda b, pt: (b, 0)),
                  pl.BlockSpec((None, t_new, past + PAGE), lambda b, pt: (b, 0, 0)),
                  pl.BlockSpec((None, t_new, PAGE), lambda b, pt: (b, 0, 0)),
                  pl.BlockSpec((B_HEADS * t_new, width), lambda b, pt: (0, 0)),
                  pl.BlockSpec((B_HEADS * t_new, PAGE), lambda b, pt: (0, 0)),
                  pl.BlockSpec(memory_space=pl.ANY),
                  pl.BlockSpec(memory_space=pl.ANY)],
        out_specs=pl.BlockSpec((t_new, B_HEADS * B_HEAD_DIM), lambda b, pt: (b, 0)),
        scratch_shapes=[pltpu.VMEM((2, width, _KVB), F32),
                        pltpu.VMEM((2, width, _KVB), F32),
                        pltpu.SemaphoreType.DMA((2, 2))])
    return pl.pallas_call(
        functools.partial(_dsa_sample_attn_kernel, n_seq=n_seq, n_pages=n_pages, g_pages=g_pages, t_new=t_new),
        grid_spec=grid_spec,
        out_shape=jax.ShapeDtypeStruct((n_seq * t_new, B_HEADS * B_HEAD_DIM), F32),
        compiler_params=_params(56, ("arbitrary",)),
    )(pt_flat, c_far, q_all, k_new, v_new, keys, thr, b_last, b_new, cache_k, cache_v)


def _merge_kernel(h_ref, oa_ref, ob_ref, wga_ref, wgb_ref, woa_ref, wob_ref, o_ref):
    h = h_ref[...]
    ga = jax.nn.sigmoid(_dot(h, wga_ref[...]))
    gb = jax.nn.sigmoid(_dot(h, wgb_ref[...]))
    a = _dot(oa_ref[...].astype(BF), woa_ref[...])
    b = _dot(ob_ref[...].astype(BF), wob_ref[...])
    o_ref[...] = (ga * a + gb * b).astype(o_ref.dtype)


def _merge(g, h, o_a, o_b, w_ga, w_gb, w_oa, w_ob, tn=512):
    d = h.shape[1]
    n = w_ga.shape[1]
    ka, kb = o_a.shape[1], o_b.shape[1]
    wspec = lambda k: pl.BlockSpec((k, tn), lambda j, i: (0, j))
    return pl.pallas_call(
        _merge_kernel, grid=(n // tn, g.m // g.tm),
        in_specs=[pl.BlockSpec((g.tm, d), lambda j, i: (i, 0)),
                  pl.BlockSpec((g.tm, ka), lambda j, i: (i, 0)),
                  pl.BlockSpec((g.tm, kb), lambda j, i: (i, 0)),
                  wspec(d), wspec(d), wspec(ka), wspec(kb)],
        out_specs=pl.BlockSpec((g.tm, tn), lambda j, i: (i, j)),
        out_shape=jax.ShapeDtypeStruct((g.m, n), BF),
        compiler_params=_params(48, ("arbitrary", "arbitrary")),
    )(h, o_a, o_b, w_ga, w_gb, w_oa, w_ob)


def _split_bf16(x):
    hi = x.astype(BF)
    return hi, (x - hi.astype(F32)).astype(BF)


def _outproj_kernel(x_ref, mg_ref, w_ref, gt_ref, g_ref, sc_ref, sh_ref, wrh_ref, wrl_ref, br_ref,
                    x1_ref, h2_ref, lt_ref):
    x1 = x_ref[...] + gt_ref[...] * _dot(mg_ref[...], w_ref[...])
    x1_ref[...] = x1
    r = lax.rsqrt(jnp.mean(x1 * x1, axis=-1, keepdims=True) + EPS)
    h2 = x1 * r * g_ref[...] * (1.0 + sc_ref[...]) + sh_ref[...]
    h2_ref[...] = h2
    hi, lo = _split_bf16(h2)
    lt_ref[...] = (_dot_nt(wrh_ref[...], hi) + _dot_nt(wrh_ref[...], lo) + _dot_nt(wrl_ref[...], hi)) + br_ref[...]


def _outproj(g, x, merged, w_out, norm_ffn, wr_hi, wr_lo, b_r):
    d = x.shape[1]
    nr = wr_hi.shape[0]
    tm = min(g.tm, 256)
    row = pl.BlockSpec((tm, d), lambda j, i: (i, 0))
    vec = pl.BlockSpec((1, d), lambda j, i: (0, 0))
    return pl.pallas_call(
        _outproj_kernel, grid=(1, g.m // tm),
        in_specs=[row, row, pl.BlockSpec((d, d), lambda j, i: (0, 0)), g.mod_spec(2, tm), vec, g.mod_spec(4, tm),
                  g.mod_spec(3, tm),
                  pl.BlockSpec((nr, d), lambda j, i: (0, 0)), pl.BlockSpec((nr, d), lambda j, i: (0, 0)),
                  pl.BlockSpec((nr, 1), lambda j, i: (0, 0))],
        out_specs=[row, row, pl.BlockSpec((nr, tm), lambda j, i: (0, i))],
        out_shape=[jax.ShapeDtypeStruct((g.m, d), F32), jax.ShapeDtypeStruct((g.m, d), F32),
                   jax.ShapeDtypeStruct((nr, g.m), F32)],
        compiler_params=_params(56, ("arbitrary", "arbitrary")),
    )(x, merged, w_out, g.mod, norm_ffn.reshape(1, d), g.mod, g.mod, wr_hi, wr_lo, b_r)


def _router_kernel(lt_ref, ids_ref, wts_ref):
    lt = lt_ref[...]
    n = lt.shape[1]
    gl = [lt[i:i + 1, :] for i in range(N_GROUPS)]
    gmax = functools.reduce(jnp.maximum, gl)
    g_sel = jnp.full((1, n), N_GROUPS - 1, I32)
    for i in range(N_GROUPS - 2, -1, -1):
        g_sel = jnp.where(gl[i] == gmax, i, g_sel)
    g_w = 1.0 / functools.reduce(lambda a, b: a + b, [jnp.exp(x - gmax) for x in gl])
    el = []
    for j in range(EPG):
        v = lt[N_GROUPS + j:N_GROUPS + j + 1, :]
        for gi in range(1, N_GROUPS):
            v = jnp.where(g_sel == gi, lt[N_GROUPS + gi * EPG + j:N_GROUPS + gi * EPG + j + 1, :], v)
        el.append(v)
    v1 = functools.reduce(jnp.maximum, el)
    i1 = jnp.full((1, n), EPG - 1, I32)
    for j in range(EPG - 2, -1, -1):
        i1 = jnp.where(el[j] == v1, j, i1)
    rest = [jnp.where(i1 == j, -jnp.inf, el[j]) for j in range(EPG)]
    v2 = functools.reduce(jnp.maximum, rest)
    i2 = jnp.full((1, n), EPG - 1, I32)
    for j in range(EPG - 2, -1, -1):
        i2 = jnp.where((rest[j] == v2) & (i1 != j), j, i2)
    e2 = jnp.exp(v2 - v1)
    den = 1.0 + e2
    zi = jnp.zeros((1, n), I32)
    zf = jnp.zeros((1, n), F32)
    ids_ref[...] = jnp.concatenate([g_sel * EPG + i1, g_sel * EPG + i2] + [zi] * 6, axis=0)
    wts_ref[...] = jnp.concatenate([(1.0 / den) * g_w, (e2 / den) * g_w] + [zf] * 6, axis=0)


def _router(lt, tn):
    nr, m = lt.shape
    return pl.pallas_call(
        _router_kernel, grid=(m // tn,),
        in_specs=[pl.BlockSpec((nr, tn), lambda i: (0, i))],
        out_specs=[pl.BlockSpec((8, tn), lambda i: (0, i)), pl.BlockSpec((8, tn), lambda i: (0, i))],
        out_shape=[jax.ShapeDtypeStruct((8, m), I32), jax.ShapeDtypeStruct((8, m), F32)],
        compiler_params=_params(32, ("arbitrary",)),
    )(lt)


def _row_copies(idx_ref, base, n_rows, src, buf, slot, sem):
    def one(r):
        return pltpu.make_async_copy(src.at[pl.ds(idx_ref[base + r], 1)], buf.at[slot, pl.ds(r, 1)], sem.at[slot])
    return one


def _moe_kernel(texp_ref, nused_ref, rtok_ref, h_hbm, gate_ref, w1_ref, w3_ref, w2_ref, o_ref,
                x_buf, sem, *, tm, n_tiles):
    i = pl.program_id(0)
    n_used = nused_ref[0]
    slot = i % 2

    def start(tile, slot_):
        cp = _row_copies(rtok_ref, tile * tm, tm, h_hbm, x_buf, slot_, sem)

        def body(r, c):
            cp(r).start()
            return c
        lax.fori_loop(0, tm, body, 0)

    def wait(tile, slot_):
        cp = _row_copies(rtok_ref, tile * tm, tm, h_hbm, x_buf, slot_, sem)

        def body(r, c):
            cp(r).wait()
            return c
        lax.fori_loop(0, tm, body, 0)

    @pl.when((i == 0) & (n_used > 0))
    def _():
        start(0, 0)

    @pl.when(i + 1 < n_used)
    def _():
        start(i + 1, 1 - slot)

    @pl.when(i < n_used)
    def _():
        wait(i, slot)
        x = x_buf[slot].astype(BF)
        a = _dot(x, w1_ref[...])
        b = _dot(x, w3_ref[...])
        hid = (a * jax.nn.sigmoid(a) * b).astype(BF)
        o_ref[...] = _dot(hid, w2_ref[...]) * gate_ref[...]

    @pl.when(i >= n_used)
    def _():
        o_ref[...] = jnp.zeros_like(o_ref)


def _moe(tile_exp, n_used, row_tok, row_gate, h2, w1, w3, w2, tm):
    p_max = row_tok.shape[0]
    n_tiles = p_max // tm
    d = h2.shape[1]
    de = w1.shape[2]
    grid_spec = pltpu.PrefetchScalarGridSpec(
        num_scalar_prefetch=3, grid=(n_tiles,),
        in_specs=[pl.BlockSpec(memory_space=pl.ANY),
                  pl.BlockSpec((tm, 1), lambda i, te, nu, rt: (i, 0)),
                  pl.BlockSpec((None, d, de), lambda i, te, nu, rt: (te[i], 0, 0)),
                  pl.BlockSpec((None, d, de), lambda i, te, nu, rt: (te[i], 0, 0)),
                  pl.BlockSpec((None, de, d), lambda i, te, nu, rt: (te[i], 0, 0))],
        out_specs=pl.BlockSpec((tm, d), lambda i, te, nu, rt: (i, 0)),
        scratch_shapes=[pltpu.VMEM((2, tm, d), F32), pltpu.SemaphoreType.DMA((2,))])
    return pl.pallas_call(
        functools.partial(_moe_kernel, tm=tm, n_tiles=n_tiles), grid_spec=grid_spec,
        out_shape=jax.ShapeDtypeStruct((p_max, d), F32),
        compiler_params=_params(56, ("arbitrary",)),
    )(tile_exp, n_used, row_tok, h2, row_gate, w1, w3, w2)


def _combine_kernel(pos_ref, x1_ref, gt_ref, gn_ref, ys_hbm, o_ref, y_buf, sem, *, tm, m):
    i = pl.program_id(0)
    n_steps = pl.num_programs(0)
    slot = i % 2

    def copies(tile, slot_):
        def one(r):
            k = r // tm
            tokr = r - k * tm
            return pltpu.make_async_copy(ys_hbm.at[pl.ds(pos_ref[k * m + tile * tm + tokr], 1)],
                                         y_buf.at[slot_, pl.ds(r, 1)], sem.at[slot_])
        return one

    def start(tile, slot_):
        cp = copies(tile, slot_)

        def body(r, c):
            cp(r).start()
            return c
        lax.fori_loop(0, 2 * tm, body, 0)

    @pl.when(i == 0)
    def _():
        start(0, 0)

    @pl.when(i + 1 < n_steps)
    def _():
        start(i + 1, 1 - slot)

    cp = copies(i, slot)

    def wbody(r, c):
        cp(r).wait()
        return c
    lax.fori_loop(0, 2 * tm, wbody, 0)

    y = y_buf[slot]
    x2 = x1_ref[...] + gt_ref[...] * (y[:tm] + y[tm:])
    r = lax.rsqrt(jnp.mean(x2 * x2, axis=-1, keepdims=True) + EPS)
    o_ref[...] = x2 * r * gn_ref[...]


def _combine(g, pos, x1, ys, norm_final):
    d = x1.shape[1]
    tm = min(g.tm, 256)
    grid_spec = pltpu.PrefetchScalarGridSpec(
        num_scalar_prefetch=1, grid=(g.m // tm,),
        in_specs=[pl.BlockSpec((tm, d), lambda i, p: (i, 0)),
                  g.mod_spec_1d(5, tm),
                  pl.BlockSpec((1, d), lambda i, p: (0, 0)),
                  pl.BlockSpec(memory_space=pl.ANY)],
        out_specs=pl.BlockSpec((tm, d), lambda i, p: (i, 0)),
        scratch_shapes=[pltpu.VMEM((2, 2 * tm, d), F32), pltpu.SemaphoreType.DMA((2,))])
    return pl.pallas_call(
        functools.partial(_combine_kernel, tm=tm, m=g.m), grid_spec=grid_spec,
        out_shape=jax.ShapeDtypeStruct((g.m, d), F32),
        compiler_params=_params(56, ("arbitrary",)),
    )(pos, x1, g.mod, norm_final.reshape(1, d), ys)


def _sort_by_expert(ids, wts, m, tm):
    e_all = jnp.concatenate([ids[0], ids[1]])
    w_all = jnp.concatenate([wts[0], wts[1]])
    tok_all = jnp.concatenate([jnp.arange(m, dtype=I32)] * 2)
    onehot = (e_all[:, None] == jnp.arange(N_EXPERTS, dtype=I32)[None, :]).astype(I32)
    rank = jnp.sum((jnp.cumsum(onehot, axis=0) - onehot) * onehot, axis=1)
    counts = jnp.sum(onehot, axis=0)
    padded = ((counts + tm - 1) // tm) * tm
    ends = jnp.cumsum(padded)
    pos = (ends - padded)[e_all] + rank
    p_max = (2 * m // tm + N_EXPERTS) * tm
    row_tok = jnp.zeros((p_max,), I32).at[pos].set(tok_all)
    row_gate = jnp.zeros((p_max,), F32).at[pos].set(w_all)
    n_used = (ends[-1] // tm).astype(I32)
    tile_start = jnp.arange(p_max // tm, dtype=I32) * tm
    tile_exp = jnp.minimum(jnp.searchsorted(ends, tile_start, side="right"), N_EXPERTS - 1).astype(I32)
    last = tile_exp[jnp.maximum(n_used - 1, 0)]
    tile_exp = jnp.where(jnp.arange(p_max // tm) < n_used, tile_exp, last)
    return pos.astype(I32), row_tok, row_gate.reshape(p_max, 1), tile_exp, n_used.reshape(1)


def _pick(m, pref):
    t = min(pref, m)
    while m % t:
        t //= 2
    return t


def kernel(x_prompt, x_sample, c_prompt, c_sample, cache_ckv, cache_krope, cache_k, cache_v, cache_kidx, page_table, w_in, kv_norm, kidx_norm, w_uk, w_uv, w_o_a, w_o_b, w_out, rel_bias, norm_attn, norm_ffn, norm_final, w_ada, b_ada, w_grp, b_grp, w_er, b_er, w1, w3, w2):
    nb, s, d = x_prompt.shape
    ns, t_new, _ = x_sample.shape
    n_pages = page_table.shape[1]
    past = n_pages * PAGE
    depth = w_in.shape[0]
    mp, ms = nb * s, ns * t_new
    pt_flat = page_table.reshape(-1).astype(I32)
    g_pages = _pick(n_pages, 16)

    cos_p, sin_p = _rope_tables(jnp.arange(s, dtype=I32))
    cos_s, sin_s = _rope_tables(past + jnp.arange(t_new, dtype=I32))
    tm_p, tm_s = _pick(mp, 512), _pick(ms, 512)
    cos_s, sin_s = jnp.tile(cos_s, (tm_s // t_new, 1)), jnp.tile(sin_s, (tm_s // t_new, 1))

    xp = x_prompt.reshape(mp, d)
    xs = x_sample.reshape(ms, d)
    mc = ((nb + ns + 15) // 16) * 16
    c_all = jnp.concatenate([c_prompt, c_sample, jnp.zeros((mc - nb - ns, d), F32)], axis=0)
    rows_p, rows_s = [], []

    for layer in range(depth):
        mod = _adaln(c_all, w_ada[layer], b_ada[layer])
        mod_p = mod[:nb].reshape(nb, 1, 6 * d)
        mod_s = jnp.repeat(mod[nb:nb + ns], t_new, axis=0)

        def make_group(m, tm, mod_arr, per_row, cos1, sin1, act_dtype, seq_len):
            if per_row:
                spec = lambda k, tm1=tm: pl.BlockSpec((tm1, d), lambda j, i: (i, k))
                spec1 = lambda k, tm1: pl.BlockSpec((tm1, d), lambda i, p: (i, k))
            else:
                spec = lambda k, tm1=tm: pl.BlockSpec((None, 1, d), lambda j, i: ((i * tm1) // seq_len, 0, k))
                spec1 = lambda k, tm1: pl.BlockSpec((None, 1, d), lambda i, p: ((i * tm1) // seq_len, 0, k))
            g = _Group(m, tm, mod_arr, spec, cos1, sin1, jnp.tile(cos1, (1, A_HEADS)), jnp.tile(sin1, (1, A_HEADS)),
                       act_dtype)
            g.mod_spec_1d = spec1
            return g

        gp = make_group(mp, tm_p, mod_p, False, cos_p, sin_p, BF, s)
        gs = make_group(ms, tm_s, mod_s, True, cos_s, sin_s, F32, t_new)

        wl = w_in[layer]
        offs = np.cumsum([0, A_HEADS * (A_NOPE + A_ROPE), KV_LORA, A_ROPE, B_HEADS * B_HEAD_DIM, _KVB, _KVB,
                          IDX_HEADS * IDX_DIM, IDX_DIM, IDX_HEADS, d, d])
        seg = lambda k: wl[:, offs[k]:offs[k + 1]]
        w_qa = seg(0).reshape(d, A_HEADS, A_NOPE + A_ROPE)
        w_nope = w_qa[:, :, :A_NOPE].reshape(d, A_HEADS * A_NOPE)
        w_qr = w_qa[:, :, A_NOPE:]
        half = A_ROPE // 2
        swap = lambda w: jnp.concatenate([w[..., half:], w[..., :half]], axis=-1)
        w_qrope = w_qr.reshape(d, A_HEADS * A_ROPE).astype(BF)
        w_qrope_sw = swap(w_qr).reshape(d, A_HEADS * A_ROPE).astype(BF)
        w_q = jnp.concatenate([w_nope, seg(3), seg(6)], axis=1).astype(BF)
        a_scale = (A_NOPE + A_ROPE) ** -0.5
        q_scale = jnp.concatenate([jnp.full((A_HEADS * A_NOPE,), a_scale, F32),
                                   jnp.full((B_HEADS * B_HEAD_DIM,), B_HEAD_DIM ** -0.5, F32),
                                   jnp.ones((IDX_HEADS * IDX_DIM,), F32)])
        w_small = jnp.concatenate([seg(1), seg(2), swap(seg(2)), seg(4), seg(5), seg(7), seg(8),
                                   jnp.zeros((d, _SM_COLS - _SM_WI - IDX_HEADS), F32)], axis=1).astype(BF)
        w_ga, w_gb = seg(9).astype(BF), seg(10).astype(BF)
        w_up = jnp.concatenate([w_uk[layer].reshape(KV_LORA, A_HEADS * A_NOPE),
                                w_uv[layer].reshape(KV_LORA, A_HEADS * A_V)], axis=1).astype(BF)
        w_uk_t = jnp.transpose(w_uk[layer], (1, 2, 0)).astype(BF)
        w_uv_h = jnp.transpose(w_uv[layer], (1, 0, 2)).astype(BF)
        w_oa, w_ob, w_o = w_o_a[layer].astype(BF), w_o_b[layer].astype(BF), w_out[layer].astype(BF)
        n_r = 128
        w_r = jnp.concatenate([w_grp[layer], w_er[layer], jnp.zeros((d, n_r - N_GROUPS - N_EXPERTS), F32)], axis=1).T
        wr_hi = w_r.astype(BF)
        wr_lo = (w_r - wr_hi.astype(F32)).astype(BF)
        b_r = jnp.concatenate([b_grp[layer], b_er[layer], jnp.zeros((n_r - N_GROUPS - N_EXPERTS,), F32)]).reshape(n_r, 1)
        w1b, w3b, w2b = w1[layer].astype(BF), w3[layer].astype(BF), w2[layer].astype(BF)

        def mixer_inputs(g, x):
            h = _modulate(g, x, norm_attn[layer], 1, 0)
            small = _proj_small(g, h, w_small, kv_norm[layer], kidx_norm[layer])
            q_all = _proj_scale(g.m, g.tm, h, w_q, q_scale, g.act_dtype)
            q_rope = _proj_rope(g, h, w_qrope, w_qrope_sw, a_scale)
            return h, small, q_all, q_rope

        h_p, sm_p, q_p, qr_p = mixer_inputs(gp, xp)
        ckv_p, kr_p, kb_p, vb_p, ki_p, wi_p, ckvb_p, krb_p, kbb_p, vbb_p, kib_p = sm_p
        kv_up = _proj_scale(mp, tm_p, ckvb_p, w_up, jnp.ones((w_up.shape[1],), F32), BF)
        t_attn = _pick(s, 512)
        oa_p = _mla_prompt(nb, s, q_p, qr_p, kv_up, krb_p, t_attn)
        ob_p = _dsa_prompt(nb, s, q_p, wi_p, kib_p, kbb_p, vbb_p, rel_bias, _pick(s, 256))

        h_s, sm_s, q_s, qr_s = mixer_inputs(gs, xs)
        ckv_s, kr_s, kb_s, vb_s, ki_s, wi_s = sm_s[:6]
        q_lat = _head_matmul(q_s, pl.BlockSpec((ms, A_NOPE), lambda h: (0, h)), w_uk_t,
                             jax.ShapeDtypeStruct((A_HEADS, ms, KV_LORA), F32),
                             pl.BlockSpec((None, ms, KV_LORA), lambda h: (h, 0, 0)))
        o_lat = _mla_sample(pt_flat, q_lat, qr_s, ckv_s, kr_s, cache_ckv[layer], cache_krope[layer],
                            ns, t_new, n_pages, g_pages)
        oa_s = _head_matmul(o_lat, pl.BlockSpec((None, ms, KV_LORA), lambda h: (h, 0, 0)), w_uv_h,
                            jax.ShapeDtypeStruct((ms, A_HEADS * A_V), F32),
                            pl.BlockSpec((ms, A_V), lambda h: (0, h)))
        n_sel_s = min(TOPK_MAX, (past + t_new) // 4)
        qcols = (A_HEADS * A_NOPE) // (B_HEADS * B_HEAD_DIM)
        qi_s = q_s[:, A_HEADS * A_NOPE + B_HEADS * B_HEAD_DIM:]
        keys_s, thr_s = _dsa_sample_index(pt_flat, qi_s, wi_s, ki_s, cache_kidx[layer], ns, t_new, n_pages, g_pages,
                                          n_sel_s)
        ob_s = _dsa_sample_attn(pt_flat, q_s, qcols, kb_s, vb_s, keys_s, thr_s, rel_bias,
                                cache_k[layer].reshape(-1, PAGE, _KVB), cache_v[layer].reshape(-1, PAGE, _KVB),
                                ns, t_new, n_pages, g_pages)

        outs = []
        for g, x, h, o_a, o_b in ((gp, xp, h_p, oa_p, ob_p), (gs, xs, h_s, oa_s, ob_s)):
            merged = _merge(g, h, o_a, o_b, w_ga, w_gb, w_oa, w_ob)
            outs.append(_outproj(g, x, merged, w_o, norm_ffn[layer], wr_hi, wr_lo, b_r))
        (x1_p, h2_p, lt_p), (x1_s, h2_s, lt_s) = outs

        m_all = mp + ms
        lt = jnp.concatenate([lt_p, lt_s], axis=1)
        h2 = jnp.concatenate([h2_p, h2_s], axis=0)
        ids, wts = _router(lt, _pick(m_all, 1024))
        tm_e = 256
        pos, row_tok, row_gate, tile_exp, n_used = _sort_by_expert(ids, wts, m_all, tm_e)
        ys = _moe(tile_exp, n_used, row_tok, row_gate, h2, w1b, w3b, w2b, tm_e)
        pos_p = jnp.concatenate([pos[:mp], pos[m_all:m_all + mp]])
        pos_s = jnp.concatenate([pos[mp:m_all], pos[m_all + mp:]])
        last = layer == depth - 1
        assert last, "stacked layers would need the un-normalised residual stream"
        xp = _combine(gp, pos_p, x1_p, ys, norm_final)
        xs = _combine(gs, pos_s, x1_s, ys, norm_final)
        rows_p.append((ckv_p, kr_p, kb_p, vb_p, ki_p))
        rows_s.append((ckv_s, kr_s, kb_s, vb_s, ki_s))

    def stack(rows, k, shape):
        return jnp.stack([r[k].reshape(shape) for r in rows])

    shp, shs = (nb, s), (ns, t_new)
    kv4 = (B_KV_HEADS, B_HEAD_DIM)
    return (xp.reshape(nb, s, d), xs.reshape(ns, t_new, d),
            stack(rows_p, 0, shp + (KV_LORA,)), stack(rows_p, 1, shp + (A_ROPE,)),
            stack(rows_p, 2, shp + kv4), stack(rows_p, 3, shp + kv4), stack(rows_p, 4, shp + (IDX_DIM,)),
            stack(rows_s, 0, shs + (KV_LORA,)), stack(rows_s, 1, shs + (A_ROPE,)),
            stack(rows_s, 2, shs + kv4), stack(rows_s, 3, shs + kv4), stack(rows_s, 4, shs + (IDX_DIM,)))
```

```python
import functools
import math

import jax
import jax.numpy as jnp
import numpy as np
from jax import lax
from jax.experimental import pallas as pl
from jax.experimental.pallas import tpu as pltpu

F32 = jnp.float32
BF = jnp.bfloat16
I32 = jnp.int32

D_MODEL = 2048
PAGE = 128
A_HEADS, A_NOPE, A_ROPE, A_V, KV_LORA = 8, 128, 64, 128, 512
ROPE_THETA = 10000.0
B_HEADS, B_KV_HEADS, B_HEAD_DIM = 8, 2, 128
B_REP = B_HEADS // B_KV_HEADS
IDX_HEADS, IDX_DIM, TOPK_MAX = 16, 64, 256
REL_BUCKETS, REL_MAX_DIST = 32, 128
N_GROUPS, EPG, TOP_K_EXPERTS, D_EXPERT = 4, 8, 2, 512
N_EXPERTS = N_GROUPS * EPG
EPS = 1e-6

LOG2E = math.log2(math.e)
NEG_BIG = -1e30
INT_MIN = -(2 ** 31)
KEY_NEG_INF = int(np.array(np.float32(-np.inf)).view(np.int32) ^ np.int32(0x7FFFFFFF))

NT_DIMS = (((1,), (1,)), ((), ()))


def _params(vmem_mb, sem):
    return pltpu.CompilerParams(vmem_limit_bytes=vmem_mb * 1024 * 1024, dimension_semantics=sem)


def _dot(a, b):
    return jnp.dot(a, b, preferred_element_type=F32)


def _dot_nt(a, b):
    return lax.dot_general(a, b, NT_DIMS, preferred_element_type=F32)


def _float_key(x):
    bits = lax.bitcast_convert_type(x, I32)
    return bits ^ ((bits >> 31) & jnp.int32(0x7FFFFFFF))


def _adaln_kernel(c_ref, w_ref, b_ref, o_ref):
    c = c_ref[...]
    s = (c * jax.nn.sigmoid(c)).astype(BF)
    o_ref[...] = _dot(s, w_ref[...].astype(BF)) + b_ref[...]


def _adaln(c_all, w_ada, b_ada):
    mc, d = c_all.shape
    n = w_ada.shape[1]
    tn = 1024
    return pl.pallas_call(
        _adaln_kernel, grid=(n // tn,),
        in_specs=[pl.BlockSpec((mc, d), lambda j: (0, 0)),
                  pl.BlockSpec((d, tn), lambda j: (0, j)),
                  pl.BlockSpec((1, tn), lambda j: (0, j))],
        out_specs=pl.BlockSpec((mc, tn), lambda j: (0, j)),
        out_shape=jax.ShapeDtypeStruct((mc, n), F32),
        compiler_params=_params(48, ("arbitrary",)),
    )(c_all, w_ada, b_ada.reshape(1, n))


class _Group:
    def __init__(self, m, tm, mod, mod_spec, cos1, sin1, cos8, sin8, act_dtype):
        self.m, self.tm = m, tm
        self.mod, self.mod_spec = mod, mod_spec
        self.cos1, self.sin1, self.cos8, self.sin8 = cos1, sin1, cos8, sin8
        self.act_dtype = act_dtype

    def tab_spec(self, tab):
        nt = tab.shape[0] // self.tm
        return pl.BlockSpec((self.tm, tab.shape[1]), lambda j, i: (i % nt, 0))


def _rope_tables(pos):
    half = A_ROPE // 2
    inv = ROPE_THETA ** (-jnp.arange(half, dtype=F32) / half)
    ang = pos.astype(F32)[:, None] * inv[None, :]
    cos, sin = jnp.cos(ang), jnp.sin(ang)
    return jnp.concatenate([cos, cos], -1), jnp.concatenate([-sin, sin], -1)


def _modulate_kernel(x_ref, g_ref, sc_ref, sh_ref, o_ref):
    x = x_ref[...]
    r = lax.rsqrt(jnp.mean(x * x, axis=-1, keepdims=True) + EPS)
    y = x * r * g_ref[...]
    o_ref[...] = (y * (1.0 + sc_ref[...]) + sh_ref[...]).astype(o_ref.dtype)


def _modulate(g, x, gain, k_scale, k_shift):
    d = x.shape[1]
    row = pl.BlockSpec((g.tm, d), lambda j, i: (i, 0))
    return pl.pallas_call(
        _modulate_kernel, grid=(1, g.m // g.tm),
        in_specs=[row, pl.BlockSpec((1, d), lambda j, i: (0, 0)), g.mod_spec(k_scale), g.mod_spec(k_shift)],
        out_specs=row, out_shape=jax.ShapeDtypeStruct((g.m, d), BF),
        compiler_params=_params(48, ("arbitrary", "arbitrary")),
    )(x, gain.reshape(1, d), g.mod, g.mod)


def _proj_scale_kernel(x_ref, w_ref, s_ref, o_ref):
    o_ref[...] = (_dot(x_ref[...], w_ref[...]) * s_ref[...]).astype(o_ref.dtype)


def _proj_scale(m, tm, x, w, scale, out_dtype, tn=512):
    k, n = w.shape
    return pl.pallas_call(
        _proj_scale_kernel, grid=(n // tn, m // tm),
        in_specs=[pl.BlockSpec((tm, k), lambda j, i: (i, 0)),
                  pl.BlockSpec((k, tn), lambda j, i: (0, j)),
                  pl.BlockSpec((1, tn), lambda j, i: (0, j))],
        out_specs=pl.BlockSpec((tm, tn), lambda j, i: (i, j)),
        out_shape=jax.ShapeDtypeStruct((m, n), out_dtype),
        compiler_params=_params(48, ("arbitrary", "arbitrary")),
    )(x, w, scale.reshape(1, n))


def _proj_rope_kernel(x_ref, w_ref, wr_ref, cos_ref, sin_ref, o_ref, *, scale):
    x = x_ref[...]
    r = (_dot(x, w_ref[...]) * cos_ref[...] + _dot(x, wr_ref[...]) * sin_ref[...]) * scale
    for h in range(A_HEADS):
        o_ref[h] = r[:, h * A_ROPE:(h + 1) * A_ROPE].astype(o_ref.dtype)


def _proj_rope(g, x, w, wr, scale):
    k, n = w.shape
    return pl.pallas_call(
        functools.partial(_proj_rope_kernel, scale=scale), grid=(1, g.m // g.tm),
        in_specs=[pl.BlockSpec((g.tm, k), lambda j, i: (i, 0)),
                  pl.BlockSpec((k, n), lambda j, i: (0, 0)),
                  pl.BlockSpec((k, n), lambda j, i: (0, 0)),
                  g.tab_spec(g.cos8), g.tab_spec(g.sin8)],
        out_specs=pl.BlockSpec((A_HEADS, g.tm, A_ROPE), lambda j, i: (0, i, 0)),
        out_shape=jax.ShapeDtypeStruct((A_HEADS, g.m, A_ROPE), g.act_dtype),
        compiler_params=_params(48, ("arbitrary", "arbitrary")),
    )(x, w, wr, g.cos8, g.sin8)


_SM_CKV, _SM_KR, _SM_KRR = 0, KV_LORA, KV_LORA + A_ROPE
_SM_KB = KV_LORA + 2 * A_ROPE
_KVB = B_KV_HEADS * B_HEAD_DIM
_SM_VB = _SM_KB + _KVB
_SM_KI = _SM_VB + _KVB
_SM_WI = _SM_KI + IDX_DIM
_SM_COLS = 1280


def _proj_small_kernel(x_ref, w_ref, cos_ref, sin_ref, kvn_ref, kin_ref,
                       ckv_ref, kr_ref, kb_ref, vb_ref, ki_ref, wi_ref,
                       ckvb_ref, krb_ref, kbb_ref, vbb_ref, kib_ref):
    a = _dot(x_ref[...], w_ref[...])
    c = a[:, _SM_CKV:_SM_CKV + KV_LORA]
    ckv = c * lax.rsqrt(jnp.mean(c * c, axis=-1, keepdims=True) + EPS) * kvn_ref[...]
    kr = (a[:, _SM_KR:_SM_KR + A_ROPE] * cos_ref[...] + a[:, _SM_KRR:_SM_KRR + A_ROPE] * sin_ref[...])
    kb = a[:, _SM_KB:_SM_KB + _KVB]
    vb = a[:, _SM_VB:_SM_VB + _KVB]
    k = a[:, _SM_KI:_SM_KI + IDX_DIM]
    ki = k * lax.rsqrt(jnp.mean(k * k, axis=-1, keepdims=True) + EPS) * kin_ref[...]
    wi = a[:, _SM_WI:_SM_WI + IDX_HEADS] * (IDX_HEADS ** -0.5 * IDX_DIM ** -0.5)
    ckv_ref[...] = ckv
    kr_ref[...] = kr
    kb_ref[...] = kb
    vb_ref[...] = vb
    ki_ref[...] = ki
    wi_ref[...] = wi
    ckvb_ref[...] = ckv.astype(BF)
    krb_ref[...] = kr.astype(BF)
    kbb_ref[...] = kb.astype(BF)
    vbb_ref[...] = vb.astype(BF)
    kib_ref[...] = ki.astype(BF)


def _proj_small(g, x, w, kv_norm, kidx_norm):
    k = x.shape[1]
    widths = (KV_LORA, A_ROPE, _KVB, _KVB, IDX_DIM, IDX_HEADS, KV_LORA, A_ROPE, _KVB, _KVB, IDX_DIM)
    dtypes = (F32,) * 6 + (BF,) * 5
    return pl.pallas_call(
        _proj_small_kernel, grid=(1, g.m // g.tm),
        in_specs=[pl.BlockSpec((g.tm, k), lambda j, i: (i, 0)),
                  pl.BlockSpec((k, _SM_COLS), lambda j, i: (0, 0)),
                  g.tab_spec(g.cos1), g.tab_spec(g.sin1),
                  pl.BlockSpec((1, KV_LORA), lambda j, i: (0, 0)),
                  pl.BlockSpec((1, IDX_DIM), lambda j, i: (0, 0))],
        out_specs=[pl.BlockSpec((g.tm, wd), lambda j, i: (i, 0)) for wd in widths],
        out_shape=[jax.ShapeDtypeStruct((g.m, wd), dt) for wd, dt in zip(widths, dtypes)],
        compiler_params=_params(48, ("arbitrary", "arbitrary")),
    )(x, w, g.cos1, g.sin1, kv_norm.reshape(1, -1), kidx_norm.reshape(1, -1))


def _mla_prompt_kernel(qn_ref, qr_ref, kn_ref, kr_ref, v_ref, o_ref, *, t):
    qi = pl.program_id(2)
    qn = qn_ref[...]
    qr = qr_ref[...]

    def chunk(c, carry, diag):
        m, l, acc = carry
        off = pl.multiple_of(c * t, t)
        s = _dot_nt(qn, kn_ref[pl.ds(off, t), :]) + _dot_nt(qr, kr_ref[pl.ds(off, t), :])
        if diag:
            row = lax.broadcasted_iota(I32, (t, t), 0)
            col = lax.broadcasted_iota(I32, (t, t), 1)
            s = jnp.where(col <= row, s, NEG_BIG)
        m_new = jnp.maximum(m, jnp.max(s, axis=-1, keepdims=True))
        p = jnp.exp2(s - m_new)
        corr = jnp.exp2(m - m_new)
        l = l * corr + jnp.sum(p, axis=-1, keepdims=True)
        acc = acc * corr + _dot(p.astype(BF), v_ref[pl.ds(off, t), :])
        return m_new, l, acc

    init = (jnp.full((t, 1), NEG_BIG, F32), jnp.zeros((t, 1), F32), jnp.zeros((t, A_V), F32))
    carry = chunk(qi, init, True)
    m, l, acc = lax.fori_loop(0, qi, lambda c, cr: chunk(c, cr, False), carry)
    o_ref[...] = (acc / l).astype(o_ref.dtype)


def _mla_prompt(nb, s, q_all, q_rope, kv_up, kr_bf, t):
    nq = s // t
    return pl.pallas_call(
        functools.partial(_mla_prompt_kernel, t=t), grid=(nb, A_HEADS, nq),
        in_specs=[pl.BlockSpec((t, A_NOPE), lambda b, h, i: (b * nq + i, h)),
                  pl.BlockSpec((None, t, A_ROPE), lambda b, h, i: (h, b * nq + i, 0)),
                  pl.BlockSpec((s, A_NOPE), lambda b, h, i: (b, h)),
                  pl.BlockSpec((s, A_ROPE), lambda b, h, i: (b, 0)),
                  pl.BlockSpec((s, A_V), lambda b, h, i: (b, A_HEADS + h))],
        out_specs=pl.BlockSpec((t, A_V), lambda b, h, i: (b * nq + i, h)),
        out_shape=jax.ShapeDtypeStruct((nb * s, A_HEADS * A_V), BF),
        compiler_params=_params(48, ("arbitrary", "arbitrary", "arbitrary")),
    )(q_all, q_rope, kv_up, kr_bf, kv_up)


def _kth_largest_key(count_ge, n_rows, k):
    def cond(carry):
        b, done, _, _ = carry
        return (b < 32) & (done == 0)

    def body(carry):
        b, _, lo, cnt_lo = carry
        cand = lo + lax.shift_left(jnp.int32(1), jnp.int32(31) - b)
        cnt = count_ge(cand)
        ok = cnt >= float(k)
        lo = jnp.where(ok, cand, lo)
        cnt_lo = jnp.where(ok, cnt, cnt_lo)
        done = (jnp.max(jnp.abs(cnt_lo - float(k))) == 0.0).astype(I32)
        return b + 1, done, lo, cnt_lo

    init = (jnp.int32(0), jnp.int32(0), jnp.full((n_rows, 1), INT_MIN, I32), jnp.full((n_rows, 1), 1e9, F32))
    _, _, lo, cnt_lo = lax.while_loop(cond, body, init)
    return lo, cnt_lo


def _tie_cutoff(count_eq_below, n_rows, need, n_bits):
    def body(b, tcur):
        cand = tcur + lax.shift_left(jnp.int32(1), jnp.int32(n_bits - 1) - b)
        return jnp.where(count_eq_below(cand) <= need, cand, tcur)

    return lax.fori_loop(0, n_bits, body, jnp.zeros((n_rows, 1), I32))


def _fold_lanes(x):
    acc = x[:, :128]
    for j in range(1, x.shape[1] // 128):
        acc = acc + x[:, j * 128:(j + 1) * 128]
    return acc


def _dsa_prompt_kernel(cfar_ref, qi_ref, wi_ref, ki_ref, qb_ref, kb_ref, vb_ref, bd_ref, bp_ref,
                       o_ref, key_ref, m_ref, l_ref, acc_ref, *, t, n_sel):
    qi = pl.program_id(1)
    n_valid = qi + 1
    row = lax.broadcasted_iota(I32, (t, t), 0)
    col = lax.broadcasted_iota(I32, (t, t), 1)
    causal = col <= row

    q_i = qi_ref[...]
    w_i = wi_ref[...]
    q_heads = [q_i[:, h * IDX_DIM:(h + 1) * IDX_DIM] for h in range(IDX_HEADS)]
    w_heads = [w_i[:, h:h + 1] for h in range(IDX_HEADS)]

    def score_chunk(c, diag):
        kc = ki_ref[pl.ds(pl.multiple_of(c * t, t), t), :]
        acc = jnp.zeros((t, t), F32)
        for h in range(IDX_HEADS):
            acc = acc + jnp.maximum(_dot_nt(q_heads[h], kc), 0.0) * w_heads[h]
        if diag:
            acc = jnp.where(causal, acc, -jnp.inf)
        key_ref[c] = _float_key(acc)

    score_chunk(qi, True)

    def score_body(c, carry):
        score_chunk(c, False)
        return carry

    lax.fori_loop(0, qi, score_body, 0)

    def count_where(pred):
        def body(c, acc):
            return acc + _fold_lanes(jnp.where(pred(key_ref[c], c), 1.0, 0.0))
        part = lax.fori_loop(0, n_valid, body, jnp.zeros((t, 128), F32))
        return jnp.sum(part, axis=-1, keepdims=True)

    thr, cnt_ge = _kth_largest_key(lambda cand: count_where(lambda kc, c: kc >= cand), t, n_sel)
    tie = (cnt_ge > float(n_sel)) & (thr > KEY_NEG_INF)

    @pl.when(jnp.max(jnp.where(tie, 1.0, 0.0)) > 0.0)
    def _():
        cnt_gt = count_where(lambda kc, c: kc > thr)
        need = jnp.where(tie, float(n_sel) - cnt_gt, 1e9)
        n_bits = int(math.ceil(math.log2(key_ref.shape[0] * t))) + 1
        tstar = _tie_cutoff(
            lambda cand: count_where(lambda kc, c: (kc == thr) & (col + c * t < cand)),
            t, need, n_bits)

        def demote(c, carry):
            kc = key_ref[c]
            key_ref[c] = jnp.where((kc == thr) & (col + c * t >= tstar) & tie, thr - 1, kc)
            return carry

        lax.fori_loop(0, n_valid, demote, 0)

    q_b = qb_ref[...]
    q_groups = [jnp.concatenate([q_b[:, (g * B_REP + r) * B_HEAD_DIM:(g * B_REP + r + 1) * B_HEAD_DIM]
                                 for r in range(B_REP)], axis=0) for g in range(B_KV_HEADS)]

    m_ref[...] = jnp.full(m_ref.shape, NEG_BIG, F32)
    l_ref[...] = jnp.zeros(l_ref.shape, F32)
    acc_ref[...] = jnp.zeros(acc_ref.shape, F32)

    def attend(c, kind):
        off = pl.multiple_of(c * t, t)
        sel = key_ref[c] >= thr
        if kind == "diag":
            sel = sel & causal
        kc = kb_ref[pl.ds(off, t), :]
        vc = vb_ref[pl.ds(off, t), :]
        for g in range(B_KV_HEADS):
            s_all = _dot_nt(q_groups[g], kc[:, g * B_HEAD_DIM:(g + 1) * B_HEAD_DIM])
            vg = vc[:, g * B_HEAD_DIM:(g + 1) * B_HEAD_DIM]
            for r in range(B_REP):
                h = g * B_REP + r
                s = s_all[r * t:(r + 1) * t]
                if kind == "far":
                    s = s + cfar_ref[h]
                elif kind == "prev":
                    s = s + bp_ref[h]
                else:
                    s = s + bd_ref[h]
                s = jnp.where(sel, s, NEG_BIG)
                m = m_ref[h]
                m_new = jnp.maximum(m, jnp.max(s, axis=-1, keepdims=True))
                p = jnp.exp2(s - jnp.tile(m_new, (1, t // 128)))
                corr = jnp.exp2(m - m_new)
                m_ref[h] = m_new
                l_ref[h] = l_ref[h] * corr + jnp.sum(p, axis=-1, keepdims=True)
                acc_ref[h] = acc_ref[h] * corr + _dot(p.astype(BF), vg)

    def attend_body(kind):
        def body(c, carry):
            attend(c, kind)
            return carry
        return body

    attend(qi, "diag")
    lax.fori_loop(jnp.maximum(qi - 1, 0), qi, attend_body("prev"), 0)
    lax.fori_loop(0, jnp.maximum(qi - 1, 0), attend_body("far"), 0)
    for h in range(B_HEADS):
        o_ref[:, h * B_HEAD_DIM:(h + 1) * B_HEAD_DIM] = (acc_ref[h] / l_ref[h]).astype(o_ref.dtype)


def _t5_bucket(dist):
    n = jnp.maximum(dist, 0)
    max_exact = REL_BUCKETS // 2
    nf = jnp.maximum(n, 1).astype(F32)
    large = max_exact + (jnp.log(nf / max_exact) / math.log(REL_MAX_DIST / max_exact)
                         * (REL_BUCKETS - max_exact)).astype(I32)
    large = jnp.minimum(large, REL_BUCKETS - 1)
    return jnp.where(n < max_exact, n, large)


def _bias_by_distance(rel_bias, n):
    return (rel_bias[_t5_bucket(jnp.arange(n, dtype=I32))].astype(F32) * LOG2E).T


def _toeplitz(vec, t):
    nh = vec.shape[0]
    fp = jnp.concatenate([vec[:, t::-1], jnp.zeros((nh, t), vec.dtype), vec[:, 2 * t - 1:t:-1]], axis=1)
    return jnp.tile(fp, (1, t))[:, :t * (3 * t - 1)].reshape(nh, t, 3 * t - 1)[:, :, :2 * t]


def _dsa_prompt(nb, s, q_all, wi, ki_bf, kb_bf, vb_bf, rel_bias, t):
    assert t + 1 >= REL_MAX_DIST
    nq = s // t
    n_sel = min(TOPK_MAX, s // 4)
    near = _toeplitz(_bias_by_distance(rel_bias, 2 * t), t)
    b_prev, b_diag = near[:, :, :t], near[:, :, t:]
    c_far = rel_bias[REL_BUCKETS - 1].astype(F32) * LOG2E
    qcols = (A_HEADS * A_NOPE) // (B_HEADS * B_HEAD_DIM)
    return pl.pallas_call(
        functools.partial(_dsa_prompt_kernel, t=t, n_sel=n_sel), grid=(nb, nq),
        in_specs=[pl.BlockSpec(memory_space=pltpu.SMEM),
                  pl.BlockSpec((t, IDX_HEADS * IDX_DIM), lambda b, i: (b * nq + i, qcols + 1)),
                  pl.BlockSpec((t, IDX_HEADS), lambda b, i: (b * nq + i, 0)),
                  pl.BlockSpec((s, IDX_DIM), lambda b, i: (b, 0)),
                  pl.BlockSpec((t, B_HEADS * B_HEAD_DIM), lambda b, i: (b * nq + i, qcols)),
                  pl.BlockSpec((s, _KVB), lambda b, i: (b, 0)),
                  pl.BlockSpec((s, _KVB), lambda b, i: (b, 0)),
                  pl.BlockSpec((B_HEADS, t, t), lambda b, i: (0, 0, 0)),
                  pl.BlockSpec((B_HEADS, t, t), lambda b, i: (0, 0, 0))],
        out_specs=pl.BlockSpec((t, B_HEADS * B_HEAD_DIM), lambda b, i: (b * nq + i, 0)),
        out_shape=jax.ShapeDtypeStruct((nb * s, B_HEADS * B_HEAD_DIM), BF),
        scratch_shapes=[pltpu.VMEM((nq, t, t), I32),
                        pltpu.VMEM((B_HEADS, t, 128), F32),
                        pltpu.VMEM((B_HEADS, t, 128), F32),
                        pltpu.VMEM((B_HEADS, t, B_HEAD_DIM), F32)],
        compiler_params=_params(56, ("arbitrary", "arbitrary")),
    )(c_far, q_all, wi, ki_bf, q_all, kb_bf, vb_bf, b_diag, b_prev)


def _page_dst(buf, slot, p, src):
    rows, cols = src.shape[1], src.shape[2]
    if buf.shape[2] == cols:
        return buf.at[slot, pl.ds(p * rows, rows)]
    return buf.at[slot, :, pl.ds(p * cols, cols)]


def _page_copies(pt_ref, seq, chunk, slot, n_pages, g_pages, srcs, bufs, sems):
    copies = []
    for p in range(g_pages):
        page = pt_ref[seq * n_pages + chunk * g_pages + p]
        for k, (src, buf) in enumerate(zip(srcs, bufs)):
            copies.append(pltpu.make_async_copy(src.at[page], _page_dst(buf, slot, p, src), sems.at[k, slot]))
    return copies


def _paged_loop(pt_ref, n_seq, n_pages, g_pages, srcs, bufs, sems, body, carry):
    b = pl.program_id(0)
    n_chunks = n_pages // g_pages
    copies = functools.partial(_page_copies, pt_ref, n_pages=n_pages, g_pages=g_pages,
                               srcs=srcs, bufs=bufs, sems=sems)

    @pl.when(b == 0)
    def _():
        for cp in copies(0, 0, 0):
            cp.start()

    for c in range(n_chunks):
        gidx = b * n_chunks + c
        slot = gidx % 2
        if c + 1 < n_chunks:
            for cp in copies(b, c + 1, 1 - slot):
                cp.start()
        else:
            @pl.when(b + 1 < n_seq)
            def _():
                for cp in copies(b + 1, 0, 1 - slot):
                    cp.start()
        for cp in copies(b, c, slot):
            cp.wait()
        carry = body(c, slot, carry)
    return carry


def _softmax_step(carry, s, v_bf):
    m, l, acc = carry
    m_new = jnp.maximum(m, jnp.max(s, axis=-1, keepdims=True))
    p = jnp.exp2(s - m_new)
    corr = jnp.exp2(m - m_new)
    l = l * corr + jnp.sum(p, axis=-1, keepdims=True)
    acc = acc * corr + _dot(p.astype(BF), v_bf)
    return m_new, l, acc


def _pad_rows(x, n):
    return jnp.concatenate([x, jnp.zeros((n - x.shape[0], x.shape[1]), x.dtype)], axis=0)


def _mla_sample_kernel(pt_ref, ql_ref, qr_ref, cn_ref, kn_ref, ck_hbm, kr_hbm, o_ref,
                       ck_buf, kr_buf, sems, *, n_seq, n_pages, g_pages, t_new):
    rows = A_HEADS * t_new
    ql = ql_ref[...].reshape(rows, KV_LORA).astype(BF)
    qr = qr_ref[...].reshape(rows, A_ROPE).astype(BF)

    cn = _pad_rows(cn_ref[...], PAGE).astype(BF)
    kn = _pad_rows(kn_ref[...], PAGE).astype(BF)
    s = _dot_nt(ql, cn) + _dot_nt(qr, kn)
    tok = lax.broadcasted_iota(I32, (rows, PAGE), 0) % t_new
    col = lax.broadcasted_iota(I32, (rows, PAGE), 1)
    s = jnp.where(col <= tok, s, NEG_BIG)
    init = (jnp.full((rows, 1), NEG_BIG, F32), jnp.zeros((rows, 1), F32), jnp.zeros((rows, KV_LORA), F32))
    carry = _softmax_step(init, s, cn)

    def body(c, slot, carry):
        ck = ck_buf[slot].astype(BF)
        kr_t = kr_buf[slot].astype(BF)
        return _softmax_step(carry, _dot_nt(ql, ck) + _dot(qr, kr_t), ck)

    m, l, acc = _paged_loop(pt_ref, n_seq, n_pages, g_pages, (ck_hbm, kr_hbm), (ck_buf, kr_buf), sems, body, carry)
    o_ref[...] = (acc / l).reshape(A_HEADS, t_new, KV_LORA)


def _mla_sample(pt_flat, q_lat, q_rope, ckv_new, kr_new, cache_ckv, cache_krope, n_seq, t_new, n_pages, g_pages):
    grid_spec = pltpu.PrefetchScalarGridSpec(
        num_scalar_prefetch=1, grid=(n_seq,),
        in_specs=[pl.BlockSpec((A_HEADS, t_new, KV_LORA), lambda b, pt: (0, b, 0)),
                  pl.BlockSpec((A_HEADS, t_new, A_ROPE), lambda b, pt: (0, b, 0)),
                  pl.BlockSpec((t_new, KV_LORA), lambda b, pt: (b, 0)),
                  pl.BlockSpec((t_new, A_ROPE), lambda b, pt: (b, 0)),
                  pl.BlockSpec(memory_space=pl.ANY),
                  pl.BlockSpec(memory_space=pl.ANY)],
        out_specs=pl.BlockSpec((A_HEADS, t_new, KV_LORA), lambda b, pt: (0, b, 0)),
        scratch_shapes=[pltpu.VMEM((2, g_pages * PAGE, KV_LORA), F32),
                        pltpu.VMEM((2, A_ROPE, g_pages * PAGE), F32),
                        pltpu.SemaphoreType.DMA((2, 2))])
    return pl.pallas_call(
        functools.partial(_mla_sample_kernel, n_seq=n_seq, n_pages=n_pages, g_pages=g_pages, t_new=t_new),
        grid_spec=grid_spec,
        out_shape=jax.ShapeDtypeStruct((A_HEADS, n_seq * t_new, KV_LORA), F32),
        compiler_params=_params(56, ("arbitrary",)),
    )(pt_flat, q_lat, q_rope, ckv_new, kr_new, cache_ckv, cache_krope)


def _headmm_kernel(x_ref, w_ref, o_ref):
    o_ref[...] = _dot(x_ref[...].astype(BF), w_ref[...]).astype(o_ref.dtype)


def _head_matmul(x, x_spec, w, out_shape, out_spec):
    nh, k, n = w.shape
    return pl.pallas_call(
        _headmm_kernel, grid=(nh,),
        in_specs=[x_spec, pl.BlockSpec((None, k, n), lambda h: (h, 0, 0))],
        out_specs=out_spec, out_shape=out_shape,
        compiler_params=_params(48, ("arbitrary",)),
    )(x, w)


def _dsa_sample_index_kernel(pt_ref, q_ref, w_ref, kin_ref, kidx_hbm, key_out, thr_out,
                             kidx_buf, sems, key_ref, keyn_ref, *, n_seq, sb, n_pages, g_pages, t_new, n_sel):
    b = pl.program_id(0)
    n_chunks = n_pages // g_pages
    width = g_pages * PAGE
    past = n_pages * PAGE
    rows = IDX_HEADS * t_new
    n_rows = sb * t_new
    n_units = sb * n_chunks
    total_units = n_seq * n_chunks

    def copies(unit, slot):
        return _page_copies(pt_ref, unit // n_chunks, unit % n_chunks, slot, n_pages, g_pages,
                            (kidx_hbm,), (kidx_buf,), sems)

    @pl.when(b == 0)
    def _():
        for cp in copies(0, 0):
            cp.start()

    tok = lax.broadcasted_iota(I32, (t_new, PAGE), 0)
    col1 = lax.broadcasted_iota(I32, (t_new, PAGE), 1)

    def head_sum(r):
        return jnp.sum(r.reshape(IDX_HEADS, t_new, r.shape[1]), axis=0)

    def unit_body(u, carry):
        unit = b * n_units + u
        slot = unit % 2

        @pl.when(unit + 1 < total_units)
        def _():
            for cp in copies(unit + 1, 1 - slot):
                cp.start()

        for cp in copies(unit, slot):
            cp.wait()
        j = u // n_chunks
        c = u % n_chunks
        q = q_ref[pl.ds(pl.multiple_of(j * rows, rows), rows), :].astype(BF)
        w = w_ref[pl.ds(pl.multiple_of(j * rows, rows), rows), :]
        tok_rows = pl.ds(pl.multiple_of(j * t_new, t_new), t_new)
        k_t = kidx_buf[slot].astype(BF)
        key_ref[c, tok_rows, :] = _float_key(head_sum(jnp.maximum(_dot(q, k_t), 0.0) * w))

        @pl.when(c == 0)
        def _():
            kn = _pad_rows(kin_ref[tok_rows, :], PAGE).astype(BF)
            s_new = head_sum(jnp.maximum(_dot_nt(q, kn), 0.0) * w)
            keyn_ref[tok_rows, :] = _float_key(jnp.where(col1 <= tok, s_new, -jnp.inf))

        return carry

    lax.fori_loop(0, n_units, unit_body, 0)

    coln = lax.broadcasted_iota(I32, (n_rows, PAGE), 1)
    colw = lax.broadcasted_iota(I32, (n_rows, width), 1)

    def count_where(pred):
        part = _fold_lanes(jnp.where(pred(keyn_ref[...], coln + past), 1.0, 0.0))
        for c in range(n_chunks):
            part = part + _fold_lanes(jnp.where(pred(key_ref[c], colw + c * width), 1.0, 0.0))
        return jnp.sum(part, axis=-1, keepdims=True)

    thr, cnt_ge = _kth_largest_key(lambda cand: count_where(lambda kc, ci: kc >= cand), n_rows, n_sel)
    tie = (cnt_ge > float(n_sel)) & (thr > KEY_NEG_INF)

    @pl.when(jnp.max(jnp.where(tie, 1.0, 0.0)) > 0.0)
    def _():
        cnt_gt = count_where(lambda kc, ci: kc > thr)
        need = jnp.where(tie, float(n_sel) - cnt_gt, 1e9)
        n_bits = int(math.ceil(math.log2(past + PAGE))) + 1
        tstar = _tie_cutoff(lambda cand: count_where(lambda kc, ci: (kc == thr) & (ci < cand)),
                            n_rows, need, n_bits)
        for c in range(n_chunks):
            kc = key_ref[c]
            key_ref[c] = jnp.where((kc == thr) & (colw + c * width >= tstar) & tie, thr - 1, kc)
        kc = keyn_ref[...]
        keyn_ref[...] = jnp.where((kc == thr) & (coln + past >= tstar) & tie, thr - 1, kc)

    for c in range(n_chunks):
        key_out[:, c * width:(c + 1) * width] = key_ref[c]
    key_out[:, past:past + PAGE] = keyn_ref[...]
    thr_out[...] = jnp.broadcast_to(thr, (n_rows, PAGE))


def _dsa_sample_index(pt_flat, q_i, w_i, ki_new, cache_kidx, n_seq, t_new, n_pages, g_pages, n_sel):
    past = n_pages * PAGE
    sb = _pick(n_seq, 16)
    rows = IDX_HEADS * t_new
    q_st = q_i.reshape(n_seq, t_new, IDX_HEADS, IDX_DIM).transpose(0, 2, 1, 3).reshape(n_seq * rows, IDX_DIM)
    w_st = w_i.reshape(n_seq, t_new, IDX_HEADS).transpose(0, 2, 1).reshape(n_seq * rows, 1)
    grid_spec = pltpu.PrefetchScalarGridSpec(
        num_scalar_prefetch=1, grid=(n_seq // sb,),
        in_specs=[pl.BlockSpec((sb * rows, IDX_DIM), lambda b, pt: (b, 0)),
                  pl.BlockSpec((sb * rows, 1), lambda b, pt: (b, 0)),
                  pl.BlockSpec((sb * t_new, IDX_DIM), lambda b, pt: (b, 0)),
                  pl.BlockSpec(memory_space=pl.ANY)],
        out_specs=[pl.BlockSpec((sb * t_new, past + PAGE), lambda b, pt: (b, 0)),
                   pl.BlockSpec((sb * t_new, PAGE), lambda b, pt: (b, 0))],
        scratch_shapes=[pltpu.VMEM((2, IDX_DIM, g_pages * PAGE), F32),
                        pltpu.SemaphoreType.DMA((1, 2)),
                        pltpu.VMEM((n_pages // g_pages, sb * t_new, g_pages * PAGE), I32),
                        pltpu.VMEM((sb * t_new, PAGE), I32)])
    return pl.pallas_call(
        functools.partial(_dsa_sample_index_kernel, n_seq=n_seq, sb=sb, n_pages=n_pages, g_pages=g_pages,
                          t_new=t_new, n_sel=n_sel),
        grid_spec=grid_spec,
        out_shape=[jax.ShapeDtypeStruct((n_seq * t_new, past + PAGE), I32),
                   jax.ShapeDtypeStruct((n_seq * t_new, PAGE), I32)],
        compiler_params=_params(56, ("arbitrary",)),
    )(pt_flat, q_st, w_st, ki_new, cache_kidx)


def _dsa_sample_attn_kernel(pt_ref, cfar_ref, qb_ref, kn_ref, vn_ref, key_ref, thr_ref, blast_ref, bnew_ref,
                            k_hbm, v_hbm, o_ref, k_buf, v_buf, sems, *, n_seq, n_pages, g_pages, t_new):
    n_chunks = n_pages // g_pages
    width = g_pages * PAGE
    past = n_pages * PAGE
    rows = B_REP * t_new
    q_b = qb_ref[...].astype(BF)
    thr = thr_ref[:, :1]
    q_groups = [jnp.concatenate([q_b[:, (g * B_REP + r) * B_HEAD_DIM:(g * B_REP + r + 1) * B_HEAD_DIM]
                                 for r in range(B_REP)], axis=0) for g in range(B_KV_HEADS)]
    far_cols = [jnp.concatenate([jnp.full((t_new, 1), cfar_ref[g * B_REP + r], F32) for r in range(B_REP)], axis=0)
                for g in range(B_KV_HEADS)]

    def step(carry, kv_fn, sel, bias_fn):
        sel4 = jnp.concatenate([sel] * B_REP, axis=0)
        new = []
        for g in range(B_KV_HEADS):
            k_bf, v_bf = kv_fn(g)
            s = _dot_nt(q_groups[g], k_bf) + bias_fn(g)
            new.append(_softmax_step(carry[g], jnp.where(sel4, s, NEG_BIG), v_bf))
        return tuple(new)

    tok = lax.broadcasted_iota(I32, (t_new, PAGE), 0)
    coln = lax.broadcasted_iota(I32, (t_new, PAGE), 1)
    sel_new = (key_ref[:, past:past + PAGE] >= thr) & (coln <= tok)
    init = tuple((jnp.full((rows, 1), NEG_BIG, F32), jnp.zeros((rows, 1), F32), jnp.zeros((rows, B_HEAD_DIM), F32))
                 for _ in range(B_KV_HEADS))
    kn = _pad_rows(kn_ref[...], PAGE).astype(BF)
    vn = _pad_rows(vn_ref[...], PAGE).astype(BF)
    head_cols = lambda x, g: x[:, g * B_HEAD_DIM:(g + 1) * B_HEAD_DIM]
    carry = step(init, lambda g: (head_cols(kn, g), head_cols(vn, g)), sel_new,
                 lambda g: bnew_ref[g * rows:(g + 1) * rows, :])

    def body(c, slot, carry):
        sel = key_ref[:, c * width:(c + 1) * width] >= thr
        if c == n_chunks - 1:
            bias_fn = lambda g: blast_ref[g * rows:(g + 1) * rows, :]
        else:
            bias_fn = lambda g: far_cols[g]
        kv_fn = lambda g: (k_buf[slot, pl.ds(g, width, stride=B_KV_HEADS), :].astype(BF),
                           v_buf[slot, pl.ds(g, width, stride=B_KV_HEADS), :].astype(BF))
        return step(carry, kv_fn, sel, bias_fn)

    carry = _paged_loop(pt_ref, n_seq, n_pages, g_pages, (k_hbm, v_hbm), (k_buf, v_buf), sems, body, carry)
    for g in range(B_KV_HEADS):
        m, l, acc = carry[g]
        o = acc / l
        for r in range(B_REP):
            h = g * B_REP + r
            o_ref[:, h * B_HEAD_DIM:(h + 1) * B_HEAD_DIM] = o[r * t_new:(r + 1) * t_new].astype(o_ref.dtype)


def _dsa_sample_attn(pt_flat, q_all, qcol, k_new, v_new, keys, thr, rel_bias, cache_k, cache_v,
                     n_seq, t_new, n_pages, g_pages):
    past = n_pages * PAGE
    width = g_pages * PAGE
    assert past - (n_pages - 1) * PAGE + 1 >= REL_MAX_DIST
    tq = np.arange(t_new)
    vec = _bias_by_distance(rel_bias, PAGE + t_new + 1)
    c_far = rel_bias[REL_BUCKETS - 1].astype(F32) * LOG2E
    near = vec[:, tq[:, None] + PAGE - np.arange(PAGE)[None, :]]
    far = jnp.broadcast_to(c_far[:, None, None], (B_HEADS, t_new, width - PAGE))
    b_last = jnp.concatenate([far, near], axis=2).reshape(B_HEADS * t_new, width)
    b_new = vec[:, np.maximum(tq[:, None] - np.arange(PAGE)[None, :], 0)].reshape(B_HEADS * t_new, PAGE)
    rpp = PAGE * B_KV_HEADS
    grid_spec = pltpu.PrefetchScalarGridSpec(
        num_scalar_prefetch=1, grid=(n_seq,),
        in_specs=[pl.BlockSpec(memory_space=pltpu.SMEM),
                  pl.BlockSpec((t_new, B_HEADS * B_HEAD_DIM), lambda b, pt: (b, qcol)),
                  pl.BlockSpec((t_new, _KVB), lambda b, pt: (b, 0)),
                  pl.BlockSpec((t_new, _KVB), lambda b, pt: (b, 0)),
                  pl.BlockSpec((t_new, past + PAGE), lambda b, pt: (b, 0)),
                  pl.BlockSpec((t_new, PAGE), lambda b, pt: (b, 0)),
                  pl.BlockSpec((B_HEADS * t_new, width), lambda b, pt: (0, 0)),
                  pl.BlockSpec((B_HEADS * t_new, PAGE), lambda b, pt: (0, 0)),
                  pl.BlockSpec(memory_space=pl.ANY),
                  pl.BlockSpec(memory_space=pl.ANY)],
        out_specs=pl.BlockSpec((t_new, B_HEADS * B_HEAD_DIM), lambda b, pt: (b, 0)),
        scratch_shapes=[pltpu.VMEM((2, g_pages * rpp, B_HEAD_DIM), F32),
                        pltpu.VMEM((2, g_pages * rpp, B_HEAD_DIM), F32),
                        pltpu.SemaphoreType.DMA((2, 2))])
    return pl.pallas_call(
        functools.partial(_dsa_sample_attn_kernel, n_seq=n_seq, n_pages=n_pages, g_pages=g_pages, t_new=t_new),
        grid_spec=grid_spec,
        out_shape=jax.ShapeDtypeStruct((n_seq * t_new, B_HEADS * B_HEAD_DIM), F32),
        compiler_params=_params(56, ("arbitrary",)),
    )(pt_flat, c_far, q_all, k_new, v_new, keys, thr, b_last, b_new, cache_k, cache_v)


def _merge_kernel(h_ref, oa_ref, ob_ref, wga_ref, wgb_ref, woa_ref, wob_ref, o_ref):
    h = h_ref[...]
    ga = jax.nn.sigmoid(_dot(h, wga_ref[...]))
    gb = jax.nn.sigmoid(_dot(h, wgb_ref[...]))
    a = _dot(oa_ref[...].astype(BF), woa_ref[...])
    b = _dot(ob_ref[...].astype(BF), wob_ref[...])
    o_ref[...] = (ga * a + gb * b).astype(o_ref.dtype)


def _merge(g, h, o_a, o_b, w_ga, w_gb, w_oa, w_ob, tn=512):
    d = h.shape[1]
    n = w_ga.shape[1]
    ka, kb = o_a.shape[1], o_b.shape[1]
    wspec = lambda k: pl.BlockSpec((k, tn), lambda j, i: (0, j))
    return pl.pallas_call(
        _merge_kernel, grid=(n // tn, g.m // g.tm),
        in_specs=[pl.BlockSpec((g.tm, d), lambda j, i: (i, 0)),
                  pl.BlockSpec((g.tm, ka), lambda j, i: (i, 0)),
                  pl.BlockSpec((g.tm, kb), lambda j, i: (i, 0)),
                  wspec(d), wspec(d), wspec(ka), wspec(kb)],
        out_specs=pl.BlockSpec((g.tm, tn), lambda j, i: (i, j)),
        out_shape=jax.ShapeDtypeStruct((g.m, n), BF),
        compiler_params=_params(48, ("arbitrary", "arbitrary")),
    )(h, o_a, o_b, w_ga, w_gb, w_oa, w_ob)


def _split_bf16(x):
    hi = x.astype(BF)
    return hi, (x - hi.astype(F32)).astype(BF)


def _outproj_kernel(x_ref, mg_ref, w_ref, gt_ref, g_ref, sc_ref, sh_ref, wrh_ref, wrl_ref, br_ref,
                    x1_ref, h2_ref, lt_ref):
    x1 = x_ref[...] + gt_ref[...] * _dot(mg_ref[...], w_ref[...])
    x1_ref[...] = x1
    r = lax.rsqrt(jnp.mean(x1 * x1, axis=-1, keepdims=True) + EPS)
    h2 = x1 * r * g_ref[...] * (1.0 + sc_ref[...]) + sh_ref[...]
    h2_ref[...] = h2
    hi, lo = _split_bf16(h2)
    lt_ref[...] = (_dot_nt(wrh_ref[...], hi) + _dot_nt(wrh_ref[...], lo) + _dot_nt(wrl_ref[...], hi)) + br_ref[...]


def _outproj(g, x, merged, w_out, norm_ffn, wr_hi, wr_lo, b_r):
    d = x.shape[1]
    nr = wr_hi.shape[0]
    tm = min(g.tm, 256)
    row = pl.BlockSpec((tm, d), lambda j, i: (i, 0))
    vec = pl.BlockSpec((1, d), lambda j, i: (0, 0))
    return pl.pallas_call(
        _outproj_kernel, grid=(1, g.m // tm),
        in_specs=[row, row, pl.BlockSpec((d, d), lambda j, i: (0, 0)), g.mod_spec(2, tm), vec, g.mod_spec(4, tm),
                  g.mod_spec(3, tm),
                  pl.BlockSpec((nr, d), lambda j, i: (0, 0)), pl.BlockSpec((nr, d), lambda j, i: (0, 0)),
                  pl.BlockSpec((nr, 1), lambda j, i: (0, 0))],
        out_specs=[row, row, pl.BlockSpec((nr, tm), lambda j, i: (0, i))],
        out_shape=[jax.ShapeDtypeStruct((g.m, d), F32), jax.ShapeDtypeStruct((g.m, d), F32),
                   jax.ShapeDtypeStruct((nr, g.m), F32)],
        compiler_params=_params(56, ("arbitrary", "arbitrary")),
    )(x, merged, w_out, g.mod, norm_ffn.reshape(1, d), g.mod, g.mod, wr_hi, wr_lo, b_r)


def _router_kernel(lt_ref, ids_ref, wts_ref):
    lt = lt_ref[...]
    n = lt.shape[1]
    gl = [lt[i:i + 1, :] for i in range(N_GROUPS)]
    gmax = functools.reduce(jnp.maximum, gl)
    g_sel = jnp.full((1, n), N_GROUPS - 1, I32)
    for i in range(N_GROUPS - 2, -1, -1):
        g_sel = jnp.where(gl[i] == gmax, i, g_sel)
    g_w = 1.0 / functools.reduce(lambda a, b: a + b, [jnp.exp(x - gmax) for x in gl])
    el = []
    for j in range(EPG):
        v = lt[N_GROUPS + j:N_GROUPS + j + 1, :]
        for gi in range(1, N_GROUPS):
            v = jnp.where(g_sel == gi, lt[N_GROUPS + gi * EPG + j:N_GROUPS + gi * EPG + j + 1, :], v)
        el.append(v)
    v1 = functools.reduce(jnp.maximum, el)
    i1 = jnp.full((1, n), EPG - 1, I32)
    for j in range(EPG - 2, -1, -1):
        i1 = jnp.where(el[j] == v1, j, i1)
    rest = [jnp.where(i1 == j, -jnp.inf, el[j]) for j in range(EPG)]
    v2 = functools.reduce(jnp.maximum, rest)
    i2 = jnp.full((1, n), EPG - 1, I32)
    for j in range(EPG - 2, -1, -1):
        i2 = jnp.where((rest[j] == v2) & (i1 != j), j, i2)
    e2 = jnp.exp(v2 - v1)
    den = 1.0 + e2
    zi = jnp.zeros((1, n), I32)
    zf = jnp.zeros((1, n), F32)
    ids_ref[...] = jnp.concatenate([g_sel * EPG + i1, g_sel * EPG + i2] + [zi] * 6, axis=0)
    wts_ref[...] = jnp.concatenate([(1.0 / den) * g_w, (e2 / den) * g_w] + [zf] * 6, axis=0)


def _router(lt, tn):
    nr, m = lt.shape
    return pl.pallas_call(
        _router_kernel, grid=(m // tn,),
        in_specs=[pl.BlockSpec((nr, tn), lambda i: (0, i))],
        out_specs=[pl.BlockSpec((8, tn), lambda i: (0, i)), pl.BlockSpec((8, tn), lambda i: (0, i))],
        out_shape=[jax.ShapeDtypeStruct((8, m), I32), jax.ShapeDtypeStruct((8, m), F32)],
        compiler_params=_params(32, ("arbitrary",)),
    )(lt)


def _dispatch_kernel(pos_ref, h_ref, xs_in, xs_out, sem, *, tm, m):
    del xs_in
    i = pl.program_id(0)

    def copy(r):
        k = r // tm
        tokr = r - k * tm
        return pltpu.make_async_copy(h_ref.at[pl.ds(tokr, 1)],
                                     xs_out.at[pl.ds(pos_ref[k * m + i * tm + tokr], 1)], sem)

    def start(r, c):
        copy(r).start()
        return c

    def wait(r, c):
        copy(r).wait()
        return c

    lax.fori_loop(0, TOP_K_EXPERTS * tm, start, 0)
    lax.fori_loop(0, TOP_K_EXPERTS * tm, wait, 0)


def _dispatch(pos, h2, xs, tm):
    m, d = h2.shape
    p_max = xs.shape[0]
    grid_spec = pltpu.PrefetchScalarGridSpec(
        num_scalar_prefetch=1, grid=(m // tm,),
        in_specs=[pl.BlockSpec((tm, d), lambda i, p: (i, 0)), pl.BlockSpec(memory_space=pl.ANY)],
        out_specs=pl.BlockSpec(memory_space=pl.ANY),
        scratch_shapes=[pltpu.SemaphoreType.DMA(())])
    return pl.pallas_call(
        functools.partial(_dispatch_kernel, tm=tm, m=m), grid_spec=grid_spec,
        out_shape=jax.ShapeDtypeStruct((p_max, d), F32),
        input_output_aliases={2: 0},
        compiler_params=_params(32, ("arbitrary",)),
    )(pos, h2, xs)


def _moe_kernel(texp_ref, nused_ref, x_ref, w1_ref, w3_ref, w2_ref, o_ref):
    i = pl.program_id(0)

    @pl.when(i < nused_ref[0])
    def _():
        x = x_ref[...].astype(BF)
        a = _dot(x, w1_ref[...].astype(BF))
        b = _dot(x, w3_ref[...].astype(BF))
        hid = (a * jax.nn.sigmoid(a) * b).astype(BF)
        o_ref[...] = _dot(hid, w2_ref[...].astype(BF))

    @pl.when(i >= nused_ref[0])
    def _():
        o_ref[...] = jnp.zeros_like(o_ref)


def _moe(tile_exp, n_used, xs, w1, w3, w2, tm):
    p_max, d = xs.shape
    de = w1.shape[2]
    grid_spec = pltpu.PrefetchScalarGridSpec(
        num_scalar_prefetch=2, grid=(p_max // tm,),
        in_specs=[pl.BlockSpec((tm, d), lambda i, te, nu: (i, 0)),
                  pl.BlockSpec((None, d, de), lambda i, te, nu: (te[i], 0, 0)),
                  pl.BlockSpec((None, d, de), lambda i, te, nu: (te[i], 0, 0)),
                  pl.BlockSpec((None, de, d), lambda i, te, nu: (te[i], 0, 0))],
        out_specs=pl.BlockSpec((tm, d), lambda i, te, nu: (i, 0)))
    return pl.pallas_call(
        _moe_kernel, grid_spec=grid_spec,
        out_shape=jax.ShapeDtypeStruct((p_max, d), F32),
        compiler_params=_params(56, ("arbitrary",)),
    )(tile_exp, n_used, xs, w1, w3, w2)


def _combine_kernel(pos_ref, x1_ref, w_ref, gt_ref, gn_ref, ys_hbm, o_ref, y_buf, sem, *, tm, m):
    i = pl.program_id(0)
    n_steps = pl.num_programs(0)
    slot = i % 2

    def copies(tile, slot_):
        def one(r):
            k = r // tm
            tokr = r - k * tm
            return pltpu.make_async_copy(ys_hbm.at[pl.ds(pos_ref[k * m + tile * tm + tokr], 1)],
                                         y_buf.at[slot_, pl.ds(r, 1)], sem.at[slot_])
        return one

    def start(tile, slot_):
        cp = copies(tile, slot_)

        def body(r, c):
            cp(r).start()
            return c
        lax.fori_loop(0, 2 * tm, body, 0)

    @pl.when(i == 0)
    def _():
        start(0, 0)

    @pl.when(i + 1 < n_steps)
    def _():
        start(i + 1, 1 - slot)

    cp = copies(i, slot)

    def wbody(r, c):
        cp(r).wait()
        return c
    lax.fori_loop(0, 2 * tm, wbody, 0)

    y = y_buf[slot]
    w = w_ref[...]
    x2 = x1_ref[...] + gt_ref[...] * (w[:, 0:1] * y[:tm] + w[:, 1:2] * y[tm:])
    r = lax.rsqrt(jnp.mean(x2 * x2, axis=-1, keepdims=True) + EPS)
    o_ref[...] = x2 * r * gn_ref[...]


def _combine(g, pos, wts, x1, ys, norm_final):
    d = x1.shape[1]
    tm = min(g.tm, 256)
    grid_spec = pltpu.PrefetchScalarGridSpec(
        num_scalar_prefetch=1, grid=(g.m // tm,),
        in_specs=[pl.BlockSpec((tm, d), lambda i, p: (i, 0)),
                  pl.BlockSpec((tm, TOP_K_EXPERTS), lambda i, p: (i, 0)),
                  g.mod_spec_1d(5, tm),
                  pl.BlockSpec((1, d), lambda i, p: (0, 0)),
                  pl.BlockSpec(memory_space=pl.ANY)],
        out_specs=pl.BlockSpec((tm, d), lambda i, p: (i, 0)),
        scratch_shapes=[pltpu.VMEM((2, 2 * tm, d), F32), pltpu.SemaphoreType.DMA((2,))])
    return pl.pallas_call(
        functools.partial(_combine_kernel, tm=tm, m=g.m), grid_spec=grid_spec,
        out_shape=jax.ShapeDtypeStruct((g.m, d), F32),
        compiler_params=_params(56, ("arbitrary",)),
    )(pos, x1, wts, g.mod, norm_final.reshape(1, d), ys)


def _sort_by_expert(ids, m, tm):
    e_all = jnp.concatenate([ids[0], ids[1]])
    onehot = (e_all[:, None] == jnp.arange(N_EXPERTS, dtype=I32)[None, :]).astype(I32)
    rank = jnp.sum((jnp.cumsum(onehot, axis=0) - onehot) * onehot, axis=1)
    counts = jnp.sum(onehot, axis=0)
    padded = ((counts + tm - 1) // tm) * tm
    ends = jnp.cumsum(padded)
    pos = jnp.sum(onehot * (ends - padded)[None, :], axis=1) + rank
    p_max = (2 * m // tm + N_EXPERTS) * tm
    n_used = (ends[-1] // tm).astype(I32)
    tile_start = jnp.arange(p_max // tm, dtype=I32) * tm
    tile_exp = jnp.minimum(jnp.sum((ends[None, :] <= tile_start[:, None]).astype(I32), axis=1), N_EXPERTS - 1)
    last = jnp.sum(jnp.where(jnp.arange(p_max // tm) == n_used - 1, tile_exp, 0))
    tile_exp = jnp.where(jnp.arange(p_max // tm) < n_used, tile_exp, last).astype(I32)
    return pos.astype(I32), p_max, tile_exp, n_used.reshape(1)


def _pick(m, pref):
    t = min(pref, m)
    while m % t:
        t //= 2
    return t


def kernel(x_prompt, x_sample, c_prompt, c_sample, cache_ckv, cache_krope, cache_k, cache_v, cache_kidx, page_table, w_in, kv_norm, kidx_norm, w_uk, w_uv, w_o_a, w_o_b, w_out, rel_bias, norm_attn, norm_ffn, norm_final, w_ada, b_ada, w_grp, b_grp, w_er, b_er, w1, w3, w2):
    nb, s, d = x_prompt.shape
    ns, t_new, _ = x_sample.shape
    n_pages = page_table.shape[1]
    past = n_pages * PAGE
    depth = w_in.shape[0]
    mp, ms = nb * s, ns * t_new
    pt_flat = page_table.reshape(-1).astype(I32)
    g_pages = _pick(n_pages, 16)

    cos_p, sin_p = _rope_tables(jnp.arange(s, dtype=I32))
    cos_s, sin_s = _rope_tables(past + jnp.arange(t_new, dtype=I32))
    tm_p, tm_s = _pick(mp, 512), _pick(ms, 512)
    cos_s, sin_s = jnp.tile(cos_s, (tm_s // t_new, 1)), jnp.tile(sin_s, (tm_s // t_new, 1))

    xp = x_prompt.reshape(mp, d)
    xs = x_sample.reshape(ms, d)
    mc = ((nb + ns + 15) // 16) * 16
    c_all = jnp.concatenate([c_prompt, c_sample, jnp.zeros((mc - nb - ns, d), F32)], axis=0)
    rows_p, rows_s = [], []

    for layer in range(depth):
        mod = _adaln(c_all, w_ada[layer], b_ada[layer])
        mod_p = mod[:nb].reshape(nb, 1, 6 * d)
        mod_s = jnp.repeat(mod[nb:nb + ns], t_new, axis=0)

        def make_group(m, tm, mod_arr, per_row, cos1, sin1, act_dtype, seq_len):
            if per_row:
                spec = lambda k, tm1=tm: pl.BlockSpec((tm1, d), lambda j, i: (i, k))
                spec1 = lambda k, tm1: pl.BlockSpec((tm1, d), lambda i, p: (i, k))
            else:
                spec = lambda k, tm1=tm: pl.BlockSpec((None, 1, d), lambda j, i: ((i * tm1) // seq_len, 0, k))
                spec1 = lambda k, tm1: pl.BlockSpec((None, 1, d), lambda i, p: ((i * tm1) // seq_len, 0, k))
            g = _Group(m, tm, mod_arr, spec, cos1, sin1, jnp.tile(cos1, (1, A_HEADS)), jnp.tile(sin1, (1, A_HEADS)),
                       act_dtype)
            g.mod_spec_1d = spec1
            return g

        gp = make_group(mp, tm_p, mod_p, False, cos_p, sin_p, BF, s)
        gs = make_group(ms, tm_s, mod_s, True, cos_s, sin_s, F32, t_new)

        wl = w_in[layer]
        offs = np.cumsum([0, A_HEADS * (A_NOPE + A_ROPE), KV_LORA, A_ROPE, B_HEADS * B_HEAD_DIM, _KVB, _KVB,
                          IDX_HEADS * IDX_DIM, IDX_DIM, IDX_HEADS, d, d])
        seg = lambda k: wl[:, offs[k]:offs[k + 1]]
        w_qa = seg(0).reshape(d, A_HEADS, A_NOPE + A_ROPE)
        w_nope = w_qa[:, :, :A_NOPE].reshape(d, A_HEADS * A_NOPE)
        w_qr = w_qa[:, :, A_NOPE:]
        half = A_ROPE // 2
        swap = lambda w: jnp.concatenate([w[..., half:], w[..., :half]], axis=-1)
        w_qrope = w_qr.reshape(d, A_HEADS * A_ROPE).astype(BF)
        w_qrope_sw = swap(w_qr).reshape(d, A_HEADS * A_ROPE).astype(BF)
        w_q = jnp.concatenate([w_nope, seg(3), seg(6)], axis=1).astype(BF)
        a_scale = (A_NOPE + A_ROPE) ** -0.5 * LOG2E
        q_scale = jnp.concatenate([jnp.full((A_HEADS * A_NOPE,), a_scale, F32),
                                   jnp.full((B_HEADS * B_HEAD_DIM,), B_HEAD_DIM ** -0.5 * LOG2E, F32),
                                   jnp.ones((IDX_HEADS * IDX_DIM,), F32)])
        w_small = jnp.concatenate([seg(1), seg(2), swap(seg(2)), seg(4), seg(5), seg(7), seg(8),
                                   jnp.zeros((d, _SM_COLS - _SM_WI - IDX_HEADS), F32)], axis=1).astype(BF)
        w_ga, w_gb = seg(9).astype(BF), seg(10).astype(BF)
        w_up = jnp.concatenate([w_uk[layer].reshape(KV_LORA, A_HEADS * A_NOPE),
                                w_uv[layer].reshape(KV_LORA, A_HEADS * A_V)], axis=1).astype(BF)
        w_uk_t = jnp.transpose(w_uk[layer], (1, 2, 0)).astype(BF)
        w_uv_h = jnp.transpose(w_uv[layer], (1, 0, 2)).astype(BF)
        w_oa, w_ob, w_o = w_o_a[layer].astype(BF), w_o_b[layer].astype(BF), w_out[layer].astype(BF)
        n_r = 128
        w_r = jnp.concatenate([w_grp[layer], w_er[layer], jnp.zeros((d, n_r - N_GROUPS - N_EXPERTS), F32)], axis=1).T
        wr_hi = w_r.astype(BF)
        wr_lo = (w_r - wr_hi.astype(F32)).astype(BF)
        b_r = jnp.concatenate([b_grp[layer], b_er[layer], jnp.zeros((n_r - N_GROUPS - N_EXPERTS,), F32)]).reshape(n_r, 1)
        n_pool = cache_k.shape[1]
        krope_t = jnp.transpose(cache_krope[layer], (0, 2, 1))
        kidx_t = jnp.transpose(cache_kidx[layer], (0, 2, 1))
        cache_k3 = cache_k[layer].reshape(n_pool, PAGE * B_KV_HEADS, B_HEAD_DIM)
        cache_v3 = cache_v[layer].reshape(n_pool, PAGE * B_KV_HEADS, B_HEAD_DIM)

        def mixer_inputs(g, x):
            h = _modulate(g, x, norm_attn[layer], 1, 0)
            small = _proj_small(g, h, w_small, kv_norm[layer], kidx_norm[layer])
            q_all = _proj_scale(g.m, g.tm, h, w_q, q_scale, g.act_dtype)
            q_rope = _proj_rope(g, h, w_qrope, w_qrope_sw, a_scale)
            return h, small, q_all, q_rope

        h_p, sm_p, q_p, qr_p = mixer_inputs(gp, xp)
        ckv_p, kr_p, kb_p, vb_p, ki_p, wi_p, ckvb_p, krb_p, kbb_p, vbb_p, kib_p = sm_p
        kv_up = _proj_scale(mp, tm_p, ckvb_p, w_up, jnp.ones((w_up.shape[1],), F32), BF)
        t_attn = _pick(s, 512)
        oa_p = _mla_prompt(nb, s, q_p, qr_p, kv_up, krb_p, t_attn)
        ob_p = _dsa_prompt(nb, s, q_p, wi_p, kib_p, kbb_p, vbb_p, rel_bias, _pick(s, 256))

        h_s, sm_s, q_s, qr_s = mixer_inputs(gs, xs)
        ckv_s, kr_s, kb_s, vb_s, ki_s, wi_s = sm_s[:6]
        q_lat = _head_matmul(q_s, pl.BlockSpec((ms, A_NOPE), lambda h: (0, h)), w_uk_t,
                             jax.ShapeDtypeStruct((A_HEADS, ms, KV_LORA), F32),
                             pl.BlockSpec((None, ms, KV_LORA), lambda h: (h, 0, 0)))
        o_lat = _mla_sample(pt_flat, q_lat, qr_s, ckv_s, kr_s, cache_ckv[layer], krope_t,
                            ns, t_new, n_pages, g_pages)
        oa_s = _head_matmul(o_lat, pl.BlockSpec((None, ms, KV_LORA), lambda h: (h, 0, 0)), w_uv_h,
                            jax.ShapeDtypeStruct((ms, A_HEADS * A_V), F32),
                            pl.BlockSpec((ms, A_V), lambda h: (0, h)))
        n_sel_s = min(TOPK_MAX, (past + t_new) // 4)
        qcols = (A_HEADS * A_NOPE) // (B_HEADS * B_HEAD_DIM)
        qi_s = q_s[:, A_HEADS * A_NOPE + B_HEADS * B_HEAD_DIM:]
        keys_s, thr_s = _dsa_sample_index(pt_flat, qi_s, wi_s, ki_s, kidx_t, ns, t_new, n_pages, g_pages, n_sel_s)
        ob_s = _dsa_sample_attn(pt_flat, q_s, qcols, kb_s, vb_s, keys_s, thr_s, rel_bias, cache_k3, cache_v3,
                                ns, t_new, n_pages, g_pages)

        outs = []
        for g, x, h, o_a, o_b in ((gp, xp, h_p, oa_p, ob_p), (gs, xs, h_s, oa_s, ob_s)):
            merged = _merge(g, h, o_a, o_b, w_ga, w_gb, w_oa, w_ob)
            outs.append(_outproj(g, x, merged, w_o, norm_ffn[layer], wr_hi, wr_lo, b_r))
        (x1_p, h2_p, lt_p), (x1_s, h2_s, lt_s) = outs

        m_all = mp + ms
        lt = jnp.concatenate([lt_p, lt_s], axis=1)
        ids, wts = _router(lt, _pick(m_all, 1024))
        tm_e = 256
        pos, p_max, tile_exp, n_used = _sort_by_expert(ids, m_all, tm_e)
        pos_p = jnp.concatenate([pos[:mp], pos[m_all:m_all + mp]])
        pos_s = jnp.concatenate([pos[mp:m_all], pos[m_all + mp:]])
        wts_t = wts[:TOP_K_EXPERTS].T
        x_sorted = _dispatch(pos_p, h2_p, jnp.zeros((p_max, d), F32), _pick(mp, 256))
        x_sorted = _dispatch(pos_s, h2_s, x_sorted, _pick(ms, 256))
        ys = _moe(tile_exp, n_used, x_sorted, w1[layer], w3[layer], w2[layer], tm_e)
        last = layer == depth - 1
        assert last, "stacked layers would need the un-normalised residual stream"
        xp = _combine(gp, pos_p, wts_t[:mp], x1_p, ys, norm_final)
        xs = _combine(gs, pos_s, wts_t[mp:], x1_s, ys, norm_final)
        rows_p.append((ckv_p, kr_p, kb_p, vb_p, ki_p))
        rows_s.append((ckv_s, kr_s, kb_s, vb_s, ki_s))

    def stack(rows, k, shape):
        if len(rows) == 1:
            return rows[0][k].reshape((1,) + shape)
        return jnp.stack([r[k].reshape(shape) for r in rows])

    shp, shs = (nb, s), (ns, t_new)
    kv4 = (B_KV_HEADS, B_HEAD_DIM)
    return (xp.reshape(nb, s, d), xs.reshape(ns, t_new, d),
            stack(rows_p, 0, shp + (KV_LORA,)), stack(rows_p, 1, shp + (A_ROPE,)),
            stack(rows_p, 2, shp + kv4), stack(rows_p, 3, shp + kv4), stack(rows_p, 4, shp + (IDX_DIM,)),
            stack(rows_s, 0, shs + (KV_LORA,)), stack(rows_s, 1, shs + (A_ROPE,)),
            stack(rows_s, 2, shs + kv4), stack(rows_s, 3, shs + kv4), stack(rows_s, 4, shs + (IDX_DIM,)))
```

```python
import functools
import math

import jax
import jax.numpy as jnp
import numpy as np
from jax import lax
from jax.experimental import pallas as pl
from jax.experimental.pallas import tpu as pltpu

F32 = jnp.float32
BF = jnp.bfloat16
I32 = jnp.int32

D_MODEL = 2048
PAGE = 128
A_HEADS, A_NOPE, A_ROPE, A_V, KV_LORA = 8, 128, 64, 128, 512
ROPE_THETA = 10000.0
B_HEADS, B_KV_HEADS, B_HEAD_DIM = 8, 2, 128
B_REP = B_HEADS // B_KV_HEADS
IDX_HEADS, IDX_DIM, TOPK_MAX = 16, 64, 256
REL_BUCKETS, REL_MAX_DIST = 32, 128
N_GROUPS, EPG, TOP_K_EXPERTS, D_EXPERT = 4, 8, 2, 512
N_EXPERTS = N_GROUPS * EPG
EPS = 1e-6

LOG2E = math.log2(math.e)
N_CHUNK_BUFS = 3
N_UNIT_BUFS = 6
DMA_UNROLL = 8
NEG_BIG = -1e30
INT_MIN = -(2 ** 31)
KEY_NEG_INF = int(np.array(np.float32(-np.inf)).view(np.int32) ^ np.int32(0x7FFFFFFF))

NT_DIMS = (((1,), (1,)), ((), ()))


def _params(vmem_mb, sem):
    return pltpu.CompilerParams(vmem_limit_bytes=vmem_mb * 1024 * 1024, dimension_semantics=sem)


def _dot(a, b):
    return jnp.dot(a, b, preferred_element_type=F32)


def _dot_nt(a, b):
    return lax.dot_general(a, b, NT_DIMS, preferred_element_type=F32)


def _float_key(x):
    bits = lax.bitcast_convert_type(x, I32)
    return bits ^ ((bits >> 31) & jnp.int32(0x7FFFFFFF))


def _adaln_kernel(c_ref, w_ref, b_ref, o_ref):
    c = c_ref[...]
    s = (c * jax.nn.sigmoid(c)).astype(BF)
    o_ref[...] = _dot(s, w_ref[...].astype(BF)) + b_ref[...]


def _adaln(c_all, w_ada, b_ada):
    mc, d = c_all.shape
    n = w_ada.shape[1]
    tn = 1024
    return pl.pallas_call(
        _adaln_kernel, grid=(n // tn,),
        in_specs=[pl.BlockSpec((mc, d), lambda j: (0, 0)),
                  pl.BlockSpec((d, tn), lambda j: (0, j)),
                  pl.BlockSpec((1, tn), lambda j: (0, j))],
        out_specs=pl.BlockSpec((mc, tn), lambda j: (0, j)),
        out_shape=jax.ShapeDtypeStruct((mc, n), F32),
        compiler_params=_params(48, ("arbitrary",)),
    )(c_all, w_ada, b_ada.reshape(1, n))


class _Group:
    def __init__(self, m, tm, mod, mod_spec, cos1, sin1, cos8, sin8, act_dtype):
        self.m, self.tm = m, tm
        self.mod, self.mod_spec = mod, mod_spec
        self.cos1, self.sin1, self.cos8, self.sin8 = cos1, sin1, cos8, sin8
        self.act_dtype = act_dtype

    def tab_spec(self, tab):
        nt = tab.shape[0] // self.tm
        return pl.BlockSpec((self.tm, tab.shape[1]), lambda j, i: (i % nt, 0))


def _rope_tables(pos):
    half = A_ROPE // 2
    inv = ROPE_THETA ** (-jnp.arange(half, dtype=F32) / half)
    ang = pos.astype(F32)[:, None] * inv[None, :]
    cos, sin = jnp.cos(ang), jnp.sin(ang)
    return jnp.concatenate([cos, cos], -1), jnp.concatenate([-sin, sin], -1)


def _modulate_kernel(x_ref, g_ref, sc_ref, sh_ref, o_ref):
    x = x_ref[...]
    r = lax.rsqrt(jnp.mean(x * x, axis=-1, keepdims=True) + EPS)
    y = x * r * g_ref[...]
    o_ref[...] = (y * (1.0 + sc_ref[...]) + sh_ref[...]).astype(o_ref.dtype)


def _modulate(g, x, gain, k_scale, k_shift):
    d = x.shape[1]
    row = pl.BlockSpec((g.tm, d), lambda j, i: (i, 0))
    return pl.pallas_call(
        _modulate_kernel, grid=(1, g.m // g.tm),
        in_specs=[row, pl.BlockSpec((1, d), lambda j, i: (0, 0)), g.mod_spec(k_scale), g.mod_spec(k_shift)],
        out_specs=row, out_shape=jax.ShapeDtypeStruct((g.m, d), BF),
        compiler_params=_params(48, ("arbitrary", "arbitrary")),
    )(x, gain.reshape(1, d), g.mod, g.mod)


def _proj_scale_kernel(x_ref, w_ref, s_ref, o_ref):
    o_ref[...] = (_dot(x_ref[...], w_ref[...]) * s_ref[...]).astype(o_ref.dtype)


def _proj_scale(m, tm, x, w, scale, out_dtype, tn=512):
    k, n = w.shape
    return pl.pallas_call(
        _proj_scale_kernel, grid=(n // tn, m // tm),
        in_specs=[pl.BlockSpec((tm, k), lambda j, i: (i, 0)),
                  pl.BlockSpec((k, tn), lambda j, i: (0, j)),
                  pl.BlockSpec((1, tn), lambda j, i: (0, j))],
        out_specs=pl.BlockSpec((tm, tn), lambda j, i: (i, j)),
        out_shape=jax.ShapeDtypeStruct((m, n), out_dtype),
        compiler_params=_params(48, ("arbitrary", "arbitrary")),
    )(x, w, scale.reshape(1, n))


def _proj_rope_kernel(x_ref, w_ref, wr_ref, cos_ref, sin_ref, o_ref, *, scale):
    x = x_ref[...]
    r = (_dot(x, w_ref[...]) * cos_ref[...] + _dot(x, wr_ref[...]) * sin_ref[...]) * scale
    for h in range(A_HEADS):
        o_ref[h] = r[:, h * A_ROPE:(h + 1) * A_ROPE].astype(o_ref.dtype)


def _proj_rope(g, x, w, wr, scale):
    k, n = w.shape
    return pl.pallas_call(
        functools.partial(_proj_rope_kernel, scale=scale), grid=(1, g.m // g.tm),
        in_specs=[pl.BlockSpec((g.tm, k), lambda j, i: (i, 0)),
                  pl.BlockSpec((k, n), lambda j, i: (0, 0)),
                  pl.BlockSpec((k, n), lambda j, i: (0, 0)),
                  g.tab_spec(g.cos8), g.tab_spec(g.sin8)],
        out_specs=pl.BlockSpec((A_HEADS, g.tm, A_ROPE), lambda j, i: (0, i, 0)),
        out_shape=jax.ShapeDtypeStruct((A_HEADS, g.m, A_ROPE), g.act_dtype),
        compiler_params=_params(48, ("arbitrary", "arbitrary")),
    )(x, w, wr, g.cos8, g.sin8)


_SM_CKV, _SM_KR, _SM_KRR = 0, KV_LORA, KV_LORA + A_ROPE
_SM_KB = KV_LORA + 2 * A_ROPE
_KVB = B_KV_HEADS * B_HEAD_DIM
_SM_VB = _SM_KB + _KVB
_SM_KI = _SM_VB + _KVB
_SM_WI = _SM_KI + IDX_DIM
_SM_COLS = 1280


def _proj_small_kernel(x_ref, w_ref, cos_ref, sin_ref, kvn_ref, kin_ref,
                       ckv_ref, kr_ref, kb_ref, vb_ref, ki_ref, wi_ref,
                       ckvb_ref, krb_ref, kbb_ref, vbb_ref, kib_ref):
    a = _dot(x_ref[...], w_ref[...])
    c = a[:, _SM_CKV:_SM_CKV + KV_LORA]
    ckv = c * lax.rsqrt(jnp.mean(c * c, axis=-1, keepdims=True) + EPS) * kvn_ref[...]
    kr = (a[:, _SM_KR:_SM_KR + A_ROPE] * cos_ref[...] + a[:, _SM_KRR:_SM_KRR + A_ROPE] * sin_ref[...])
    kb = a[:, _SM_KB:_SM_KB + _KVB]
    vb = a[:, _SM_VB:_SM_VB + _KVB]
    k = a[:, _SM_KI:_SM_KI + IDX_DIM]
    ki = k * lax.rsqrt(jnp.mean(k * k, axis=-1, keepdims=True) + EPS) * kin_ref[...]
    wi = a[:, _SM_WI:_SM_WI + IDX_HEADS] * (IDX_HEADS ** -0.5 * IDX_DIM ** -0.5)
    ckv_ref[...] = ckv
    kr_ref[...] = kr
    kb_ref[...] = kb
    vb_ref[...] = vb
    ki_ref[...] = ki
    wi_ref[...] = wi
    ckvb_ref[...] = ckv.astype(BF)
    krb_ref[...] = kr.astype(BF)
    kbb_ref[...] = kb.astype(BF)
    vbb_ref[...] = vb.astype(BF)
    kib_ref[...] = ki.astype(BF)


def _proj_small(g, x, w, kv_norm, kidx_norm):
    k = x.shape[1]
    widths = (KV_LORA, A_ROPE, _KVB, _KVB, IDX_DIM, IDX_HEADS, KV_LORA, A_ROPE, _KVB, _KVB, IDX_DIM)
    dtypes = (F32,) * 6 + (BF,) * 5
    return pl.pallas_call(
        _proj_small_kernel, grid=(1, g.m // g.tm),
        in_specs=[pl.BlockSpec((g.tm, k), lambda j, i: (i, 0)),
                  pl.BlockSpec((k, _SM_COLS), lambda j, i: (0, 0)),
                  g.tab_spec(g.cos1), g.tab_spec(g.sin1),
                  pl.BlockSpec((1, KV_LORA), lambda j, i: (0, 0)),
                  pl.BlockSpec((1, IDX_DIM), lambda j, i: (0, 0))],
        out_specs=[pl.BlockSpec((g.tm, wd), lambda j, i: (i, 0)) for wd in widths],
        out_shape=[jax.ShapeDtypeStruct((g.m, wd), dt) for wd, dt in zip(widths, dtypes)],
        compiler_params=_params(48, ("arbitrary", "arbitrary")),
    )(x, w, g.cos1, g.sin1, kv_norm.reshape(1, -1), kidx_norm.reshape(1, -1))


def _mla_prompt_kernel(qn_ref, qr_ref, kn_ref, kr_ref, v_ref, o_ref, *, t):
    qi = pl.program_id(2)
    qn = qn_ref[...]
    qr = qr_ref[...]

    def chunk(c, carry, diag):
        m, l, acc = carry
        off = pl.multiple_of(c * t, t)
        s = _dot_nt(qn, kn_ref[pl.ds(off, t), :]) + _dot_nt(qr, kr_ref[pl.ds(off, t), :])
        if diag:
            row = lax.broadcasted_iota(I32, (t, t), 0)
            col = lax.broadcasted_iota(I32, (t, t), 1)
            s = jnp.where(col <= row, s, NEG_BIG)
        m_new = jnp.maximum(m, jnp.max(s, axis=-1, keepdims=True))
        p = jnp.exp2(s - m_new)
        corr = jnp.exp2(m - m_new)
        l = l * corr + jnp.sum(p, axis=-1, keepdims=True)
        acc = acc * corr + _dot(p.astype(BF), v_ref[pl.ds(off, t), :])
        return m_new, l, acc

    init = (jnp.full((t, 1), NEG_BIG, F32), jnp.zeros((t, 1), F32), jnp.zeros((t, A_V), F32))
    carry = chunk(qi, init, True)
    pairs = qi // 2
    carry = lax.fori_loop(0, pairs, lambda c, cr: chunk(2 * c + 1, chunk(2 * c, cr, False), False), carry)
    m, l, acc = lax.fori_loop(2 * pairs, qi, lambda c, cr: chunk(c, cr, False), carry)
    o_ref[...] = (acc / l).astype(o_ref.dtype)


def _mla_prompt(nb, s, q_all, q_rope, kv_up, kr_bf, t):
    nq = s // t
    return pl.pallas_call(
        functools.partial(_mla_prompt_kernel, t=t), grid=(nb, A_HEADS, nq),
        in_specs=[pl.BlockSpec((t, A_NOPE), lambda b, h, i: (b * nq + i, h)),
                  pl.BlockSpec((None, t, A_ROPE), lambda b, h, i: (h, b * nq + i, 0)),
                  pl.BlockSpec((s, A_NOPE), lambda b, h, i: (b, h)),
                  pl.BlockSpec((s, A_ROPE), lambda b, h, i: (b, 0)),
                  pl.BlockSpec((s, A_V), lambda b, h, i: (b, A_HEADS + h))],
        out_specs=pl.BlockSpec((t, A_V), lambda b, h, i: (b * nq + i, h)),
        out_shape=jax.ShapeDtypeStruct((nb * s, A_HEADS * A_V), BF),
        compiler_params=_params(48, ("arbitrary", "arbitrary", "arbitrary")),
    )(q_all, q_rope, kv_up, kr_bf, kv_up)


def _kth_largest_key(count_ge, n_rows, k):
    def cond(carry):
        b, done, _, _ = carry
        return (b < 32) & (done == 0)

    def body(carry):
        b, _, lo, cnt_lo = carry
        cand = lo + lax.shift_left(jnp.int32(1), jnp.int32(31) - b)
        cnt = count_ge(cand)
        ok = cnt >= float(k)
        lo = jnp.where(ok, cand, lo)
        cnt_lo = jnp.where(ok, cnt, cnt_lo)
        done = (jnp.max(jnp.abs(cnt_lo - float(k))) == 0.0).astype(I32)
        return b + 1, done, lo, cnt_lo

    init = (jnp.int32(0), jnp.int32(0), jnp.full((n_rows, 1), INT_MIN, I32), jnp.full((n_rows, 1), 1e9, F32))
    _, _, lo, cnt_lo = lax.while_loop(cond, body, init)
    return lo, cnt_lo


def _tie_cutoff(count_eq_below, n_rows, need, n_bits):
    def body(b, tcur):
        cand = tcur + lax.shift_left(jnp.int32(1), jnp.int32(n_bits - 1) - b)
        return jnp.where(count_eq_below(cand) <= need, cand, tcur)

    return lax.fori_loop(0, n_bits, body, jnp.zeros((n_rows, 1), I32))


def _paired_loop(lo, hi, body, carry):
    pairs = (hi - lo) // 2
    carry = lax.fori_loop(0, pairs, lambda i, cr: body(lo + 2 * i + 1, body(lo + 2 * i, cr)), carry)
    return lax.fori_loop(lo + 2 * pairs, hi, body, carry)


def _fold_lanes(x):
    acc = x[:, :128]
    for j in range(1, x.shape[1] // 128):
        acc = acc + x[:, j * 128:(j + 1) * 128]
    return acc


def _dsa_prompt_kernel(cfar_ref, qi_ref, wi_ref, ki_ref, qb_ref, kb_ref, vb_ref, bd_ref, bp_ref,
                       o_ref, key_ref, m_ref, l_ref, acc_ref, *, t, n_sel):
    qi = pl.program_id(1)
    n_valid = qi + 1
    row = lax.broadcasted_iota(I32, (t, t), 0)
    col = lax.broadcasted_iota(I32, (t, t), 1)
    causal = col <= row

    q_i = qi_ref[...]
    w_i = wi_ref[...]
    q_heads = [q_i[:, h * IDX_DIM:(h + 1) * IDX_DIM] for h in range(IDX_HEADS)]
    w_heads = [w_i[:, h:h + 1] for h in range(IDX_HEADS)]

    def score_chunk(c, diag):
        kc = ki_ref[pl.ds(pl.multiple_of(c * t, t), t), :]
        acc = jnp.zeros((t, t), F32)
        for h in range(IDX_HEADS):
            acc = acc + jnp.maximum(_dot_nt(q_heads[h], kc), 0.0) * w_heads[h]
        if diag:
            acc = jnp.where(causal, acc, -jnp.inf)
        key_ref[c] = _float_key(acc)

    score_chunk(qi, True)

    def score_body(c, carry):
        score_chunk(c, False)
        return carry

    _paired_loop(0, qi, score_body, 0)

    def count_where(pred):
        def body(c, acc):
            return acc + _fold_lanes(jnp.where(pred(key_ref[c], c), 1.0, 0.0))
        part = _paired_loop(0, n_valid, body, jnp.zeros((t, 128), F32))
        return jnp.sum(part, axis=-1, keepdims=True)

    thr, cnt_ge = _kth_largest_key(lambda cand: count_where(lambda kc, c: kc >= cand), t, n_sel)
    tie = (cnt_ge > float(n_sel)) & (thr > KEY_NEG_INF)

    @pl.when(jnp.max(jnp.where(tie, 1.0, 0.0)) > 0.0)
    def _():
        cnt_gt = count_where(lambda kc, c: kc > thr)
        need = jnp.where(tie, float(n_sel) - cnt_gt, 1e9)
        n_bits = int(math.ceil(math.log2(key_ref.shape[0] * t))) + 1
        tstar = _tie_cutoff(
            lambda cand: count_where(lambda kc, c: (kc == thr) & (col + c * t < cand)),
            t, need, n_bits)

        def demote(c, carry):
            kc = key_ref[c]
            key_ref[c] = jnp.where((kc == thr) & (col + c * t >= tstar) & tie, thr - 1, kc)
            return carry

        lax.fori_loop(0, n_valid, demote, 0)

    q_b = qb_ref[...]
    q_groups = [jnp.concatenate([q_b[:, (g * B_REP + r) * B_HEAD_DIM:(g * B_REP + r + 1) * B_HEAD_DIM]
                                 for r in range(B_REP)], axis=0) for g in range(B_KV_HEADS)]

    m_ref[...] = jnp.full(m_ref.shape, NEG_BIG, F32)
    l_ref[...] = jnp.zeros(l_ref.shape, F32)
    acc_ref[...] = jnp.zeros(acc_ref.shape, F32)

    def attend(c, kind):
        off = pl.multiple_of(c * t, t)
        sel = key_ref[c] >= thr
        if kind == "diag":
            sel = sel & causal
        kc = kb_ref[pl.ds(off, t), :]
        vc = vb_ref[pl.ds(off, t), :]
        for g in range(B_KV_HEADS):
            s_all = _dot_nt(q_groups[g], kc[:, g * B_HEAD_DIM:(g + 1) * B_HEAD_DIM])
            vg = vc[:, g * B_HEAD_DIM:(g + 1) * B_HEAD_DIM]
            for r in range(B_REP):
                h = g * B_REP + r
                s = s_all[r * t:(r + 1) * t]
                if kind == "far":
                    s = s + cfar_ref[h]
                elif kind == "prev":
                    s = s + bp_ref[h]
                else:
                    s = s + bd_ref[h]
                s = jnp.where(sel, s, NEG_BIG)
                m = m_ref[h]
                m_new = jnp.maximum(m, jnp.max(s, axis=-1, keepdims=True))
                p = jnp.exp2(s - jnp.tile(m_new, (1, t // 128)))
                corr = jnp.exp2(m - m_new)
                m_ref[h] = m_new
                l_ref[h] = l_ref[h] * corr + jnp.sum(p, axis=-1, keepdims=True)
                acc_ref[h] = acc_ref[h] * corr + _dot(p.astype(BF), vg)

    def attend_body(kind):
        def body(c, carry):
            attend(c, kind)
            return carry
        return body

    attend(qi, "diag")
    lax.fori_loop(jnp.maximum(qi - 1, 0), qi, attend_body("prev"), 0)
    _paired_loop(0, jnp.maximum(qi - 1, 0), attend_body("far"), 0)
    for h in range(B_HEADS):
        o_ref[:, h * B_HEAD_DIM:(h + 1) * B_HEAD_DIM] = (acc_ref[h] / l_ref[h]).astype(o_ref.dtype)


def _t5_bucket(dist):
    n = jnp.maximum(dist, 0)
    max_exact = REL_BUCKETS // 2
    nf = jnp.maximum(n, 1).astype(F32)
    large = max_exact + (jnp.log(nf / max_exact) / math.log(REL_MAX_DIST / max_exact)
                         * (REL_BUCKETS - max_exact)).astype(I32)
    large = jnp.minimum(large, REL_BUCKETS - 1)
    return jnp.where(n < max_exact, n, large)


def _bias_by_distance(rel_bias, n):
    return (rel_bias[_t5_bucket(jnp.arange(n, dtype=I32))].astype(F32) * LOG2E).T


def _toeplitz(vec, t):
    nh = vec.shape[0]
    fp = jnp.concatenate([vec[:, t::-1], jnp.zeros((nh, t), vec.dtype), vec[:, 2 * t - 1:t:-1]], axis=1)
    return jnp.tile(fp, (1, t))[:, :t * (3 * t - 1)].reshape(nh, t, 3 * t - 1)[:, :, :2 * t]


def _dsa_prompt(nb, s, q_all, wi, ki_bf, kb_bf, vb_bf, rel_bias, t):
    assert t + 1 >= REL_MAX_DIST
    nq = s // t
    n_sel = min(TOPK_MAX, s // 4)
    near = _toeplitz(_bias_by_distance(rel_bias, 2 * t), t)
    b_prev, b_diag = near[:, :, :t], near[:, :, t:]
    c_far = rel_bias[REL_BUCKETS - 1].astype(F32) * LOG2E
    qcols = (A_HEADS * A_NOPE) // (B_HEADS * B_HEAD_DIM)
    return pl.pallas_call(
        functools.partial(_dsa_prompt_kernel, t=t, n_sel=n_sel), grid=(nb, nq),
        in_specs=[pl.BlockSpec(memory_space=pltpu.SMEM),
                  pl.BlockSpec((t, IDX_HEADS * IDX_DIM), lambda b, i: (b * nq + i, qcols + 1)),
                  pl.BlockSpec((t, IDX_HEADS), lambda b, i: (b * nq + i, 0)),
                  pl.BlockSpec((s, IDX_DIM), lambda b, i: (b, 0)),
                  pl.BlockSpec((t, B_HEADS * B_HEAD_DIM), lambda b, i: (b * nq + i, qcols)),
                  pl.BlockSpec((s, _KVB), lambda b, i: (b, 0)),
                  pl.BlockSpec((s, _KVB), lambda b, i: (b, 0)),
                  pl.BlockSpec((B_HEADS, t, t), lambda b, i: (0, 0, 0)),
                  pl.BlockSpec((B_HEADS, t, t), lambda b, i: (0, 0, 0))],
        out_specs=pl.BlockSpec((t, B_HEADS * B_HEAD_DIM), lambda b, i: (b * nq + i, 0)),
        out_shape=jax.ShapeDtypeStruct((nb * s, B_HEADS * B_HEAD_DIM), BF),
        scratch_shapes=[pltpu.VMEM((nq, t, t), I32),
                        pltpu.VMEM((B_HEADS, t, 128), F32),
                        pltpu.VMEM((B_HEADS, t, 128), F32),
                        pltpu.VMEM((B_HEADS, t, B_HEAD_DIM), F32)],
        compiler_params=_params(56, ("arbitrary", "arbitrary")),
    )(c_far, q_all, wi, ki_bf, q_all, kb_bf, vb_bf, b_diag, b_prev)


def _page_dst(buf, slot, p, src):
    rows, cols = src.shape[1], src.shape[2]
    if buf.shape[2] == cols:
        return buf.at[slot, pl.ds(p * rows, rows)]
    return buf.at[slot, :, pl.ds(p * cols, cols)]


def _page_copies(pt_ref, seq, chunk, slot, n_pages, g_pages, srcs, bufs, sems):
    copies = []
    for p in range(g_pages):
        page = pt_ref[seq * n_pages + chunk * g_pages + p]
        for k, (src, buf) in enumerate(zip(srcs, bufs)):
            copies.append(pltpu.make_async_copy(src.at[page], _page_dst(buf, slot, p, src), sems.at[k, slot]))
    return copies


def _paged_loop(pt_ref, n_seq, n_pages, g_pages, srcs, bufs, sems, body, carry):
    b = pl.program_id(0)
    n_chunks = n_pages // g_pages
    n_buf = bufs[0].shape[0]
    total = n_seq * n_chunks

    def copies(gidx, slot):
        return _page_copies(pt_ref, gidx // n_chunks, gidx % n_chunks, slot, n_pages, g_pages, srcs, bufs, sems)

    @pl.when(b == 0)
    def _():
        for a in range(min(n_buf - 1, total)):
            for cp in copies(a, a):
                cp.start()

    for c in range(n_chunks):
        gidx = b * n_chunks + c
        slot = gidx % n_buf
        nxt = gidx + (n_buf - 1)

        @pl.when(nxt < total)
        def _():
            for cp in copies(nxt, nxt % n_buf):
                cp.start()

        for cp in copies(gidx, slot):
            cp.wait()
        carry = body(c, slot, carry)
    return carry


def _softmax_step(carry, s, v_bf):
    m, l, acc = carry
    m_new = jnp.maximum(m, jnp.max(s, axis=-1, keepdims=True))
    p = jnp.exp2(s - m_new)
    corr = jnp.exp2(m - m_new)
    l = l * corr + jnp.sum(p, axis=-1, keepdims=True)
    acc = acc * corr + _dot(p.astype(BF), v_bf)
    return m_new, l, acc


def _pad_rows(x, n):
    return jnp.concatenate([x, jnp.zeros((n - x.shape[0], x.shape[1]), x.dtype)], axis=0)


def _mla_sample_kernel(pt_ref, ql_ref, qr_ref, cn_ref, kn_ref, ck_hbm, kr_hbm, o_ref,
                       ck_buf, kr_buf, sems, *, n_seq, n_pages, g_pages, t_new):
    rows = A_HEADS * t_new
    ql = ql_ref[...].reshape(rows, KV_LORA).astype(BF)
    qr = qr_ref[...].reshape(rows, A_ROPE).astype(BF)

    cn = _pad_rows(cn_ref[...], PAGE).astype(BF)
    kn = _pad_rows(kn_ref[...], PAGE).astype(BF)
    s = _dot_nt(ql, cn) + _dot_nt(qr, kn)
    tok = lax.broadcasted_iota(I32, (rows, PAGE), 0) % t_new
    col = lax.broadcasted_iota(I32, (rows, PAGE), 1)
    s = jnp.where(col <= tok, s, NEG_BIG)
    init = (jnp.full((rows, 1), NEG_BIG, F32), jnp.zeros((rows, 1), F32), jnp.zeros((rows, KV_LORA), F32))
    carry = _softmax_step(init, s, cn)

    def body(c, slot, carry):
        ck = ck_buf[slot].astype(BF)
        kr_t = kr_buf[slot].astype(BF)
        return _softmax_step(carry, _dot_nt(ql, ck) + _dot(qr, kr_t), ck)

    m, l, acc = _paged_loop(pt_ref, n_seq, n_pages, g_pages, (ck_hbm, kr_hbm), (ck_buf, kr_buf), sems, body, carry)
    o_ref[...] = (acc / l).reshape(A_HEADS, t_new, KV_LORA)


def _mla_sample(pt_flat, q_lat, q_rope, ckv_new, kr_new, cache_ckv, cache_krope, n_seq, t_new, n_pages, g_pages):
    grid_spec = pltpu.PrefetchScalarGridSpec(
        num_scalar_prefetch=1, grid=(n_seq,),
        in_specs=[pl.BlockSpec((A_HEADS, t_new, KV_LORA), lambda b, pt: (0, b, 0)),
                  pl.BlockSpec((A_HEADS, t_new, A_ROPE), lambda b, pt: (0, b, 0)),
                  pl.BlockSpec((t_new, KV_LORA), lambda b, pt: (b, 0)),
                  pl.BlockSpec((t_new, A_ROPE), lambda b, pt: (b, 0)),
                  pl.BlockSpec(memory_space=pl.ANY),
                  pl.BlockSpec(memory_space=pl.ANY)],
        out_specs=pl.BlockSpec((A_HEADS, t_new, KV_LORA), lambda b, pt: (0, b, 0)),
        scratch_shapes=[pltpu.VMEM((N_CHUNK_BUFS, g_pages * PAGE, KV_LORA), F32),
                        pltpu.VMEM((N_CHUNK_BUFS, A_ROPE, g_pages * PAGE), F32),
                        pltpu.SemaphoreType.DMA((2, N_CHUNK_BUFS))])
    return pl.pallas_call(
        functools.partial(_mla_sample_kernel, n_seq=n_seq, n_pages=n_pages, g_pages=g_pages, t_new=t_new),
        grid_spec=grid_spec,
        out_shape=jax.ShapeDtypeStruct((A_HEADS, n_seq * t_new, KV_LORA), F32),
        compiler_params=_params(56, ("arbitrary",)),
    )(pt_flat, q_lat, q_rope, ckv_new, kr_new, cache_ckv, cache_krope)


def _headmm_kernel(x_ref, w_ref, o_ref):
    o_ref[...] = _dot(x_ref[...].astype(BF), w_ref[...]).astype(o_ref.dtype)


def _head_matmul(x, x_spec, w, out_shape, out_spec):
    nh, k, n = w.shape
    return pl.pallas_call(
        _headmm_kernel, grid=(nh,),
        in_specs=[x_spec, pl.BlockSpec((None, k, n), lambda h: (h, 0, 0))],
        out_specs=out_spec, out_shape=out_shape,
        compiler_params=_params(48, ("arbitrary",)),
    )(x, w)


def _dsa_sample_index_kernel(pt_ref, q_ref, w_ref, kin_ref, kidx_hbm, key_out, thr_out,
                             kidx_buf, sems, key_ref, keyn_ref, *, n_seq, sb, n_pages, g_pages, t_new, n_sel):
    b = pl.program_id(0)
    n_chunks = n_pages // g_pages
    width = g_pages * PAGE
    past = n_pages * PAGE
    rows = IDX_HEADS * t_new
    n_rows = sb * t_new
    n_units = sb * n_chunks
    total_units = n_seq * n_chunks

    def copies(unit, slot):
        return _page_copies(pt_ref, unit // n_chunks, unit % n_chunks, slot, n_pages, g_pages,
                            (kidx_hbm,), (kidx_buf,), sems)

    n_buf = kidx_buf.shape[0]
    ug = 2 if n_units % 2 == 0 and n_buf % 2 == 0 else 1
    ahead = (n_buf // ug - 1) * ug

    @pl.when(b == 0)
    def _():
        for a in range(min(ahead, total_units)):
            for cp in copies(a, a % n_buf):
                cp.start()

    tok = lax.broadcasted_iota(I32, (t_new, PAGE), 0)
    col1 = lax.broadcasted_iota(I32, (t_new, PAGE), 1)

    def head_sum(r):
        return jnp.sum(r.reshape(IDX_HEADS, t_new, r.shape[1]), axis=0)

    def seq_rows(j):
        q = q_ref[pl.ds(pl.multiple_of(j * rows, rows), rows), :].astype(BF)
        w = w_ref[pl.ds(pl.multiple_of(j * rows, rows), rows), :]
        return q, w, pl.ds(pl.multiple_of(j * t_new, t_new), t_new)

    def score_new(j, carry):
        q, w, tok_rows = seq_rows(j)
        kn = _pad_rows(kin_ref[tok_rows, :], PAGE).astype(BF)
        s_new = head_sum(jnp.maximum(_dot_nt(q, kn), 0.0) * w)
        keyn_ref[tok_rows, :] = _float_key(jnp.where(col1 <= tok, s_new, -jnp.inf))
        return carry

    lax.fori_loop(0, sb, score_new, 0)

    def score_unit(u, slot):
        q, w, tok_rows = seq_rows(u // n_chunks)
        k_t = kidx_buf[slot].astype(BF)
        key_ref[u % n_chunks, tok_rows, :] = _float_key(head_sum(jnp.maximum(_dot(q, k_t), 0.0) * w))

    def group_body(gi, carry):
        base = b * n_units + gi * ug
        for k in range(ug):
            nxt = base + ahead + k

            @pl.when(nxt < total_units)
            def _():
                for cp in copies(nxt, nxt % n_buf):
                    cp.start()

        for k in range(ug):
            for cp in copies(base + k, (base + k) % n_buf):
                cp.wait()
        for k in range(ug):
            score_unit(gi * ug + k, (base + k) % n_buf)
        return carry

    lax.fori_loop(0, n_units // ug, group_body, 0)

    coln = lax.broadcasted_iota(I32, (n_rows, PAGE), 1)
    colw = lax.broadcasted_iota(I32, (n_rows, width), 1)

    def count_where(pred):
        part = _fold_lanes(jnp.where(pred(keyn_ref[...], coln + past), 1.0, 0.0))
        for c in range(n_chunks):
            part = part + _fold_lanes(jnp.where(pred(key_ref[c], colw + c * width), 1.0, 0.0))
        return jnp.sum(part, axis=-1, keepdims=True)

    thr, cnt_ge = _kth_largest_key(lambda cand: count_where(lambda kc, ci: kc >= cand), n_rows, n_sel)
    tie = (cnt_ge > float(n_sel)) & (thr > KEY_NEG_INF)

    @pl.when(jnp.max(jnp.where(tie, 1.0, 0.0)) > 0.0)
    def _():
        cnt_gt = count_where(lambda kc, ci: kc > thr)
        need = jnp.where(tie, float(n_sel) - cnt_gt, 1e9)
        n_bits = int(math.ceil(math.log2(past + PAGE))) + 1
        tstar = _tie_cutoff(lambda cand: count_where(lambda kc, ci: (kc == thr) & (ci < cand)),
                            n_rows, need, n_bits)
        for c in range(n_chunks):
            kc = key_ref[c]
            key_ref[c] = jnp.where((kc == thr) & (colw + c * width >= tstar) & tie, thr - 1, kc)
        kc = keyn_ref[...]
        keyn_ref[...] = jnp.where((kc == thr) & (coln + past >= tstar) & tie, thr - 1, kc)

    for c in range(n_chunks):
        key_out[:, c * width:(c + 1) * width] = key_ref[c]
    key_out[:, past:past + PAGE] = keyn_ref[...]
    thr_out[...] = jnp.broadcast_to(thr, (n_rows, PAGE))


def _dsa_sample_index(pt_flat, q_i, w_i, ki_new, cache_kidx, n_seq, t_new, n_pages, g_pages, n_sel):
    past = n_pages * PAGE
    sb = _pick(n_seq, 16)
    rows = IDX_HEADS * t_new
    q_st = q_i.reshape(n_seq, t_new, IDX_HEADS, IDX_DIM).transpose(0, 2, 1, 3).reshape(n_seq * rows, IDX_DIM)
    w_st = w_i.reshape(n_seq, t_new, IDX_HEADS).transpose(0, 2, 1).reshape(n_seq * rows, 1)
    grid_spec = pltpu.PrefetchScalarGridSpec(
        num_scalar_prefetch=1, grid=(n_seq // sb,),
        in_specs=[pl.BlockSpec((sb * rows, IDX_DIM), lambda b, pt: (b, 0)),
                  pl.BlockSpec((sb * rows, 1), lambda b, pt: (b, 0)),
                  pl.BlockSpec((sb * t_new, IDX_DIM), lambda b, pt: (b, 0)),
                  pl.BlockSpec(memory_space=pl.ANY)],
        out_specs=[pl.BlockSpec((sb * t_new, past + PAGE), lambda b, pt: (b, 0)),
                   pl.BlockSpec((sb * t_new, PAGE), lambda b, pt: (b, 0))],
        scratch_shapes=[pltpu.VMEM((N_UNIT_BUFS, IDX_DIM, g_pages * PAGE), F32),
                        pltpu.SemaphoreType.DMA((1, N_UNIT_BUFS)),
                        pltpu.VMEM((n_pages // g_pages, sb * t_new, g_pages * PAGE), I32),
                        pltpu.VMEM((sb * t_new, PAGE), I32)])
    return pl.pallas_call(
        functools.partial(_dsa_sample_index_kernel, n_seq=n_seq, sb=sb, n_pages=n_pages, g_pages=g_pages,
                          t_new=t_new, n_sel=n_sel),
        grid_spec=grid_spec,
        out_shape=[jax.ShapeDtypeStruct((n_seq * t_new, past + PAGE), I32),
                   jax.ShapeDtypeStruct((n_seq * t_new, PAGE), I32)],
        compiler_params=_params(56, ("arbitrary",)),
    )(pt_flat, q_st, w_st, ki_new, cache_kidx)


def _dsa_sample_attn_kernel(pt_ref, cfar_ref, qb_ref, kn_ref, vn_ref, key_ref, thr_ref, blast_ref, bnew_ref,
                            k_hbm, v_hbm, o_ref, k_buf, v_buf, sems, *, n_seq, n_pages, g_pages, t_new):
    n_chunks = n_pages // g_pages
    width = g_pages * PAGE
    past = n_pages * PAGE
    rows = B_REP * t_new
    q_b = qb_ref[...].astype(BF)
    thr = thr_ref[:, :1]
    q_groups = [jnp.concatenate([q_b[:, (g * B_REP + r) * B_HEAD_DIM:(g * B_REP + r + 1) * B_HEAD_DIM]
                                 for r in range(B_REP)], axis=0) for g in range(B_KV_HEADS)]
    far_cols = [jnp.concatenate([jnp.full((t_new, 1), cfar_ref[g * B_REP + r], F32) for r in range(B_REP)], axis=0)
                for g in range(B_KV_HEADS)]

    def step(carry, kv_fn, sel, bias_fn):
        sel4 = jnp.concatenate([sel] * B_REP, axis=0)
        new = []
        for g in range(B_KV_HEADS):
            k_bf, v_bf = kv_fn(g)
            s = _dot_nt(q_groups[g], k_bf) + bias_fn(g)
            new.append(_softmax_step(carry[g], jnp.where(sel4, s, NEG_BIG), v_bf))
        return tuple(new)

    tok = lax.broadcasted_iota(I32, (t_new, PAGE), 0)
    coln = lax.broadcasted_iota(I32, (t_new, PAGE), 1)
    sel_new = (key_ref[:, past:past + PAGE] >= thr) & (coln <= tok)
    init = tuple((jnp.full((rows, 1), NEG_BIG, F32), jnp.zeros((rows, 1), F32), jnp.zeros((rows, B_HEAD_DIM), F32))
                 for _ in range(B_KV_HEADS))
    kn = _pad_rows(kn_ref[...], PAGE).astype(BF)
    vn = _pad_rows(vn_ref[...], PAGE).astype(BF)
    head_cols = lambda x, g: x[:, g * B_HEAD_DIM:(g + 1) * B_HEAD_DIM]
    carry = step(init, lambda g: (head_cols(kn, g), head_cols(vn, g)), sel_new,
                 lambda g: bnew_ref[g * rows:(g + 1) * rows, :])

    def body(c, slot, carry):
        sel = key_ref[:, c * width:(c + 1) * width] >= thr
        if c == n_chunks - 1:
            bias_fn = lambda g: blast_ref[g * rows:(g + 1) * rows, :]
        else:
            bias_fn = lambda g: far_cols[g]
        kv_fn = lambda g: (k_buf[slot, pl.ds(g, width, stride=B_KV_HEADS), :].astype(BF),
                           v_buf[slot, pl.ds(g, width, stride=B_KV_HEADS), :].astype(BF))
        return step(carry, kv_fn, sel, bias_fn)

    carry = _paged_loop(pt_ref, n_seq, n_pages, g_pages, (k_hbm, v_hbm), (k_buf, v_buf), sems, body, carry)
    for g in range(B_KV_HEADS):
        m, l, acc = carry[g]
        o = acc / l
        for r in range(B_REP):
            h = g * B_REP + r
            o_ref[:, h * B_HEAD_DIM:(h + 1) * B_HEAD_DIM] = o[r * t_new:(r + 1) * t_new].astype(o_ref.dtype)


def _dsa_sample_attn(pt_flat, q_all, qcol, k_new, v_new, keys, thr, rel_bias, cache_k, cache_v,
                     n_seq, t_new, n_pages, g_pages):
    past = n_pages * PAGE
    width = g_pages * PAGE
    assert past - (n_pages - 1) * PAGE + 1 >= REL_MAX_DIST
    tq = np.arange(t_new)
    vec = _bias_by_distance(rel_bias, PAGE + t_new + 1)
    c_far = rel_bias[REL_BUCKETS - 1].astype(F32) * LOG2E
    near = vec[:, tq[:, None] + PAGE - np.arange(PAGE)[None, :]]
    far = jnp.broadcast_to(c_far[:, None, None], (B_HEADS, t_new, width - PAGE))
    b_last = jnp.concatenate([far, near], axis=2).reshape(B_HEADS * t_new, width)
    b_new = vec[:, np.maximum(tq[:, None] - np.arange(PAGE)[None, :], 0)].reshape(B_HEADS * t_new, PAGE)
    rpp = PAGE * B_KV_HEADS
    grid_spec = pltpu.PrefetchScalarGridSpec(
        num_scalar_prefetch=1, grid=(n_seq,),
        in_specs=[pl.BlockSpec(memory_space=pltpu.SMEM),
                  pl.BlockSpec((t_new, B_HEADS * B_HEAD_DIM), lambda b, pt: (b, qcol)),
                  pl.BlockSpec((t_new, _KVB), lambda b, pt: (b, 0)),
                  pl.BlockSpec((t_new, _KVB), lambda b, pt: (b, 0)),
                  pl.BlockSpec((t_new, past + PAGE), lambda b, pt: (b, 0)),
                  pl.BlockSpec((t_new, PAGE), lambda b, pt: (b, 0)),
                  pl.BlockSpec((B_HEADS * t_new, width), lambda b, pt: (0, 0)),
                  pl.BlockSpec((B_HEADS * t_new, PAGE), lambda b, pt: (0, 0)),
                  pl.BlockSpec(memory_space=pl.ANY),
                  pl.BlockSpec(memory_space=pl.ANY)],
        out_specs=pl.BlockSpec((t_new, B_HEADS * B_HEAD_DIM), lambda b, pt: (b, 0)),
        scratch_shapes=[pltpu.VMEM((N_CHUNK_BUFS, g_pages * rpp, B_HEAD_DIM), F32),
                        pltpu.VMEM((N_CHUNK_BUFS, g_pages * rpp, B_HEAD_DIM), F32),
                        pltpu.SemaphoreType.DMA((2, N_CHUNK_BUFS))])
    return pl.pallas_call(
        functools.partial(_dsa_sample_attn_kernel, n_seq=n_seq, n_pages=n_pages, g_pages=g_pages, t_new=t_new),
        grid_spec=grid_spec,
        out_shape=jax.ShapeDtypeStruct((n_seq * t_new, B_HEADS * B_HEAD_DIM), F32),
        compiler_params=_params(56, ("arbitrary",)),
    )(pt_flat, c_far, q_all, k_new, v_new, keys, thr, b_last, b_new, cache_k, cache_v)


def _merge_kernel(h_ref, oa_ref, ob_ref, wga_ref, wgb_ref, woa_ref, wob_ref, o_ref):
    h = h_ref[...]
    ga = jax.nn.sigmoid(_dot(h, wga_ref[...]))
    gb = jax.nn.sigmoid(_dot(h, wgb_ref[...]))
    a = _dot(oa_ref[...].astype(BF), woa_ref[...])
    b = _dot(ob_ref[...].astype(BF), wob_ref[...])
    o_ref[...] = (ga * a + gb * b).astype(o_ref.dtype)


def _merge(g, h, o_a, o_b, w_ga, w_gb, w_oa, w_ob, tn=512):
    d = h.shape[1]
    n = w_ga.shape[1]
    ka, kb = o_a.shape[1], o_b.shape[1]
    wspec = lambda k: pl.BlockSpec((k, tn), lambda j, i: (0, j))
    return pl.pallas_call(
        _merge_kernel, grid=(n // tn, g.m // g.tm),
        in_specs=[pl.BlockSpec((g.tm, d), lambda j, i: (i, 0)),
                  pl.BlockSpec((g.tm, ka), lambda j, i: (i, 0)),
                  pl.BlockSpec((g.tm, kb), lambda j, i: (i, 0)),
                  wspec(d), wspec(d), wspec(ka), wspec(kb)],
        out_specs=pl.BlockSpec((g.tm, tn), lambda j, i: (i, j)),
        out_shape=jax.ShapeDtypeStruct((g.m, n), BF),
        compiler_params=_params(48, ("arbitrary", "arbitrary")),
    )(h, o_a, o_b, w_ga, w_gb, w_oa, w_ob)


def _split_bf16(x):
    hi = x.astype(BF)
    return hi, (x - hi.astype(F32)).astype(BF)


def _outproj_kernel(x_ref, mg_ref, w_ref, gt_ref, g_ref, sc_ref, sh_ref, wrh_ref, wrl_ref, br_ref,
                    x1_ref, h2_ref, lt_ref):
    x1 = x_ref[...] + gt_ref[...] * _dot(mg_ref[...], w_ref[...])
    x1_ref[...] = x1
    r = lax.rsqrt(jnp.mean(x1 * x1, axis=-1, keepdims=True) + EPS)
    h2 = x1 * r * g_ref[...] * (1.0 + sc_ref[...]) + sh_ref[...]
    h2_ref[...] = h2
    hi, lo = _split_bf16(h2)
    lt_ref[...] = (_dot_nt(wrh_ref[...], hi) + _dot_nt(wrh_ref[...], lo) + _dot_nt(wrl_ref[...], hi)) + br_ref[...]


def _outproj(g, x, merged, w_out, norm_ffn, wr_hi, wr_lo, b_r):
    d = x.shape[1]
    nr = wr_hi.shape[0]
    tm = min(g.tm, 256)
    row = pl.BlockSpec((tm, d), lambda j, i: (i, 0))
    vec = pl.BlockSpec((1, d), lambda j, i: (0, 0))
    return pl.pallas_call(
        _outproj_kernel, grid=(1, g.m // tm),
        in_specs=[row, row, pl.BlockSpec((d, d), lambda j, i: (0, 0)), g.mod_spec(2, tm), vec, g.mod_spec(4, tm),
                  g.mod_spec(3, tm),
                  pl.BlockSpec((nr, d), lambda j, i: (0, 0)), pl.BlockSpec((nr, d), lambda j, i: (0, 0)),
                  pl.BlockSpec((nr, 1), lambda j, i: (0, 0))],
        out_specs=[row, row, pl.BlockSpec((nr, tm), lambda j, i: (0, i))],
        out_shape=[jax.ShapeDtypeStruct((g.m, d), F32), jax.ShapeDtypeStruct((g.m, d), F32),
                   jax.ShapeDtypeStruct((nr, g.m), F32)],
        compiler_params=_params(56, ("arbitrary", "arbitrary")),
    )(x, merged, w_out, g.mod, norm_ffn.reshape(1, d), g.mod, g.mod, wr_hi, wr_lo, b_r)


def _router_kernel(lt_ref, ids_ref, wts_ref):
    lt = lt_ref[...]
    n = lt.shape[1]
    gl = [lt[i:i + 1, :] for i in range(N_GROUPS)]
    gmax = functools.reduce(jnp.maximum, gl)
    g_sel = jnp.full((1, n), N_GROUPS - 1, I32)
    for i in range(N_GROUPS - 2, -1, -1):
        g_sel = jnp.where(gl[i] == gmax, i, g_sel)
    g_w = 1.0 / functools.reduce(lambda a, b: a + b, [jnp.exp(x - gmax) for x in gl])
    el = []
    for j in range(EPG):
        v = lt[N_GROUPS + j:N_GROUPS + j + 1, :]
        for gi in range(1, N_GROUPS):
            v = jnp.where(g_sel == gi, lt[N_GROUPS + gi * EPG + j:N_GROUPS + gi * EPG + j + 1, :], v)
        el.append(v)
    v1 = functools.reduce(jnp.maximum, el)
    i1 = jnp.full((1, n), EPG - 1, I32)
    for j in range(EPG - 2, -1, -1):
        i1 = jnp.where(el[j] == v1, j, i1)
    rest = [jnp.where(i1 == j, -jnp.inf, el[j]) for j in range(EPG)]
    v2 = functools.reduce(jnp.maximum, rest)
    i2 = jnp.full((1, n), EPG - 1, I32)
    for j in range(EPG - 2, -1, -1):
        i2 = jnp.where((rest[j] == v2) & (i1 != j), j, i2)
    e2 = jnp.exp(v2 - v1)
    den = 1.0 + e2
    zi = jnp.zeros((1, n), I32)
    zf = jnp.zeros((1, n), F32)
    ids_ref[...] = jnp.concatenate([g_sel * EPG + i1, g_sel * EPG + i2] + [zi] * 6, axis=0)
    wts_ref[...] = jnp.concatenate([(1.0 / den) * g_w, (e2 / den) * g_w] + [zf] * 6, axis=0)


def _router(lt, tn):
    nr, m = lt.shape
    return pl.pallas_call(
        _router_kernel, grid=(m // tn,),
        in_specs=[pl.BlockSpec((nr, tn), lambda i: (0, i))],
        out_specs=[pl.BlockSpec((8, tn), lambda i: (0, i)), pl.BlockSpec((8, tn), lambda i: (0, i))],
        out_shape=[jax.ShapeDtypeStruct((8, m), I32), jax.ShapeDtypeStruct((8, m), F32)],
        compiler_params=_params(32, ("arbitrary",)),
    )(lt)


def _dispatch_kernel(pos_ref, h_ref, xs_in, xs_out, sem, *, tm, m):
    del xs_in
    i = pl.program_id(0)

    def copy(k, tokr):
        return pltpu.make_async_copy(h_ref.at[pl.ds(tokr, 1)],
                                     xs_out.at[pl.ds(pos_ref[k * m + i * tm + tokr], 1)], sem)

    def each_row(fn):
        for k in range(TOP_K_EXPERTS):
            def body(tokr, c, k=k):
                fn(copy(k, tokr))
                return c
            lax.fori_loop(0, tm, body, 0, unroll=DMA_UNROLL)

    each_row(lambda cp: cp.start())
    each_row(lambda cp: cp.wait())


def _dispatch(pos, h2, xs, tm):
    m, d = h2.shape
    p_max = xs.shape[0]
    grid_spec = pltpu.PrefetchScalarGridSpec(
        num_scalar_prefetch=1, grid=(m // tm,),
        in_specs=[pl.BlockSpec((tm, d), lambda i, p: (i, 0)), pl.BlockSpec(memory_space=pl.ANY)],
        out_specs=pl.BlockSpec(memory_space=pl.ANY),
        scratch_shapes=[pltpu.SemaphoreType.DMA(())])
    return pl.pallas_call(
        functools.partial(_dispatch_kernel, tm=tm, m=m), grid_spec=grid_spec,
        out_shape=jax.ShapeDtypeStruct((p_max, d), F32),
        input_output_aliases={2: 0},
        compiler_params=_params(32, ("arbitrary",)),
    )(pos, h2, xs)


def _moe_kernel(texp_ref, nused_ref, x_ref, w1_ref, w3_ref, w2_ref, o_ref):
    i = pl.program_id(0)

    @pl.when(i < nused_ref[0])
    def _():
        x = x_ref[...].astype(BF)
        a = _dot(x, w1_ref[...].astype(BF))
        b = _dot(x, w3_ref[...].astype(BF))
        hid = (a * jax.nn.sigmoid(a) * b).astype(BF)
        o_ref[...] = _dot(hid, w2_ref[...].astype(BF))

    @pl.when(i >= nused_ref[0])
    def _():
        o_ref[...] = jnp.zeros_like(o_ref)


def _moe(tile_exp, n_used, xs, w1, w3, w2, tm):
    p_max, d = xs.shape
    de = w1.shape[2]
    grid_spec = pltpu.PrefetchScalarGridSpec(
        num_scalar_prefetch=2, grid=(p_max // tm,),
        in_specs=[pl.BlockSpec((tm, d), lambda i, te, nu: (i, 0)),
                  pl.BlockSpec((None, d, de), lambda i, te, nu: (te[i], 0, 0)),
                  pl.BlockSpec((None, d, de), lambda i, te, nu: (te[i], 0, 0)),
                  pl.BlockSpec((None, de, d), lambda i, te, nu: (te[i], 0, 0))],
        out_specs=pl.BlockSpec((tm, d), lambda i, te, nu: (i, 0)))
    return pl.pallas_call(
        _moe_kernel, grid_spec=grid_spec,
        out_shape=jax.ShapeDtypeStruct((p_max, d), F32),
        compiler_params=_params(56, ("arbitrary",)),
    )(tile_exp, n_used, xs, w1, w3, w2)


def _combine_kernel(pos_ref, x1_ref, w_ref, gt_ref, gn_ref, ys_hbm, o_ref, y_buf, sem, *, tm, m):
    i = pl.program_id(0)
    n_steps = pl.num_programs(0)
    slot = i % 2

    def each_row(tile, slot_, fn):
        for k in range(TOP_K_EXPERTS):
            def body(tokr, c, k=k):
                fn(pltpu.make_async_copy(ys_hbm.at[pl.ds(pos_ref[k * m + tile * tm + tokr], 1)],
                                         y_buf.at[slot_, pl.ds(k * tm + tokr, 1)], sem.at[slot_]))
                return c
            lax.fori_loop(0, tm, body, 0, unroll=DMA_UNROLL)

    @pl.when(i == 0)
    def _():
        each_row(0, 0, lambda cp: cp.start())

    @pl.when(i + 1 < n_steps)
    def _():
        each_row(i + 1, 1 - slot, lambda cp: cp.start())

    each_row(i, slot, lambda cp: cp.wait())

    y = y_buf[slot]
    w = w_ref[...]
    x2 = x1_ref[...] + gt_ref[...] * (w[:, 0:1] * y[:tm] + w[:, 1:2] * y[tm:])
    r = lax.rsqrt(jnp.mean(x2 * x2, axis=-1, keepdims=True) + EPS)
    o_ref[...] = x2 * r * gn_ref[...]


def _combine(g, pos, wts, x1, ys, norm_final):
    d = x1.shape[1]
    tm = min(g.tm, 256)
    grid_spec = pltpu.PrefetchScalarGridSpec(
        num_scalar_prefetch=1, grid=(g.m // tm,),
        in_specs=[pl.BlockSpec((tm, d), lambda i, p: (i, 0)),
                  pl.BlockSpec((tm, TOP_K_EXPERTS), lambda i, p: (i, 0)),
                  g.mod_spec_1d(5, tm),
                  pl.BlockSpec((1, d), lambda i, p: (0, 0)),
                  pl.BlockSpec(memory_space=pl.ANY)],
        out_specs=pl.BlockSpec((tm, d), lambda i, p: (i, 0)),
        scratch_shapes=[pltpu.VMEM((2, 2 * tm, d), F32), pltpu.SemaphoreType.DMA((2,))])
    return pl.pallas_call(
        functools.partial(_combine_kernel, tm=tm, m=g.m), grid_spec=grid_spec,
        out_shape=jax.ShapeDtypeStruct((g.m, d), F32),
        compiler_params=_params(56, ("arbitrary",)),
    )(pos, x1, wts, g.mod, norm_final.reshape(1, d), ys)


def _sort_by_expert(ids, m, tm):
    e_all = jnp.concatenate([ids[0], ids[1]])
    onehot = (e_all[:, None] == jnp.arange(N_EXPERTS, dtype=I32)[None, :]).astype(I32)
    rank = jnp.sum((jnp.cumsum(onehot, axis=0) - onehot) * onehot, axis=1)
    counts = jnp.sum(onehot, axis=0)
    padded = ((counts + tm - 1) // tm) * tm
    ends = jnp.cumsum(padded)
    pos = jnp.sum(onehot * (ends - padded)[None, :], axis=1) + rank
    p_max = (2 * m // tm + N_EXPERTS) * tm
    n_used = (ends[-1] // tm).astype(I32)
    tile_start = jnp.arange(p_max // tm, dtype=I32) * tm
    tile_exp = jnp.minimum(jnp.sum((ends[None, :] <= tile_start[:, None]).astype(I32), axis=1), N_EXPERTS - 1)
    last = jnp.sum(jnp.where(jnp.arange(p_max // tm) == n_used - 1, tile_exp, 0))
    tile_exp = jnp.where(jnp.arange(p_max // tm) < n_used, tile_exp, last).astype(I32)
    return pos.astype(I32), p_max, tile_exp, n_used.reshape(1)


def _pick(m, pref):
    t = min(pref, m)
    while m % t:
        t //= 2
    return t


def kernel(x_prompt, x_sample, c_prompt, c_sample, cache_ckv, cache_krope, cache_k, cache_v, cache_kidx, page_table, w_in, kv_norm, kidx_norm, w_uk, w_uv, w_o_a, w_o_b, w_out, rel_bias, norm_attn, norm_ffn, norm_final, w_ada, b_ada, w_grp, b_grp, w_er, b_er, w1, w3, w2):
    nb, s, d = x_prompt.shape
    ns, t_new, _ = x_sample.shape
    n_pages = page_table.shape[1]
    past = n_pages * PAGE
    depth = w_in.shape[0]
    mp, ms = nb * s, ns * t_new
    pt_flat = page_table.reshape(-1).astype(I32)
    g_pages = _pick(n_pages, 16)

    cos_p, sin_p = _rope_tables(jnp.arange(s, dtype=I32))
    cos_s, sin_s = _rope_tables(past + jnp.arange(t_new, dtype=I32))
    tm_p, tm_s = _pick(mp, 512), _pick(ms, 512)
    cos_s, sin_s = jnp.tile(cos_s, (tm_s // t_new, 1)), jnp.tile(sin_s, (tm_s // t_new, 1))

    xp = x_prompt.reshape(mp, d)
    xs = x_sample.reshape(ms, d)
    mc = ((nb + ns + 15) // 16) * 16
    c_all = jnp.concatenate([c_prompt, c_sample, jnp.zeros((mc - nb - ns, d), F32)], axis=0)
    rows_p, rows_s = [], []

    for layer in range(depth):
        mod = _adaln(c_all, w_ada[layer], b_ada[layer])
        mod_p = mod[:nb].reshape(nb, 1, 6 * d)
        mod_s = jnp.repeat(mod[nb:nb + ns], t_new, axis=0)

        def make_group(m, tm, mod_arr, per_row, cos1, sin1, act_dtype, seq_len):
            if per_row:
                spec = lambda k, tm1=tm: pl.BlockSpec((tm1, d), lambda j, i: (i, k))
                spec1 = lambda k, tm1: pl.BlockSpec((tm1, d), lambda i, p: (i, k))
            else:
                spec = lambda k, tm1=tm: pl.BlockSpec((None, 1, d), lambda j, i: ((i * tm1) // seq_len, 0, k))
                spec1 = lambda k, tm1: pl.BlockSpec((None, 1, d), lambda i, p: ((i * tm1) // seq_len, 0, k))
            g = _Group(m, tm, mod_arr, spec, cos1, sin1, jnp.tile(cos1, (1, A_HEADS)), jnp.tile(sin1, (1, A_HEADS)),
                       act_dtype)
            g.mod_spec_1d = spec1
            return g

        gp = make_group(mp, tm_p, mod_p, False, cos_p, sin_p, BF, s)
        gs = make_group(ms, tm_s, mod_s, True, cos_s, sin_s, F32, t_new)

        wl = w_in[layer]
        offs = np.cumsum([0, A_HEADS * (A_NOPE + A_ROPE), KV_LORA, A_ROPE, B_HEADS * B_HEAD_DIM, _KVB, _KVB,
                          IDX_HEADS * IDX_DIM, IDX_DIM, IDX_HEADS, d, d])
        seg = lambda k: wl[:, offs[k]:offs[k + 1]]
        w_qa = seg(0).reshape(d, A_HEADS, A_NOPE + A_ROPE)
        w_nope = w_qa[:, :, :A_NOPE].reshape(d, A_HEADS * A_NOPE)
        w_qr = w_qa[:, :, A_NOPE:]
        half = A_ROPE // 2
        swap = lambda w: jnp.concatenate([w[..., half:], w[..., :half]], axis=-1)
        w_qrope = w_qr.reshape(d, A_HEADS * A_ROPE).astype(BF)
        w_qrope_sw = swap(w_qr).reshape(d, A_HEADS * A_ROPE).astype(BF)
        w_q = jnp.concatenate([w_nope, seg(3), seg(6)], axis=1).astype(BF)
        a_scale = (A_NOPE + A_ROPE) ** -0.5 * LOG2E
        q_scale = jnp.concatenate([jnp.full((A_HEADS * A_NOPE,), a_scale, F32),
                                   jnp.full((B_HEADS * B_HEAD_DIM,), B_HEAD_DIM ** -0.5 * LOG2E, F32),
                                   jnp.ones((IDX_HEADS * IDX_DIM,), F32)])
        w_small = jnp.concatenate([seg(1), seg(2), swap(seg(2)), seg(4), seg(5), seg(7), seg(8),
                                   jnp.zeros((d, _SM_COLS - _SM_WI - IDX_HEADS), F32)], axis=1).astype(BF)
        w_ga, w_gb = seg(9).astype(BF), seg(10).astype(BF)
        w_up = jnp.concatenate([w_uk[layer].reshape(KV_LORA, A_HEADS * A_NOPE),
                                w_uv[layer].reshape(KV_LORA, A_HEADS * A_V)], axis=1).astype(BF)
        w_uk_t = jnp.transpose(w_uk[layer], (1, 2, 0)).astype(BF)
        w_uv_h = jnp.transpose(w_uv[layer], (1, 0, 2)).astype(BF)
        w_oa, w_ob, w_o = w_o_a[layer].astype(BF), w_o_b[layer].astype(BF), w_out[layer].astype(BF)
        n_r = 128
        w_r = jnp.concatenate([w_grp[layer], w_er[layer], jnp.zeros((d, n_r - N_GROUPS - N_EXPERTS), F32)], axis=1).T
        wr_hi = w_r.astype(BF)
        wr_lo = (w_r - wr_hi.astype(F32)).astype(BF)
        b_r = jnp.concatenate([b_grp[layer], b_er[layer], jnp.zeros((n_r - N_GROUPS - N_EXPERTS,), F32)]).reshape(n_r, 1)
        n_pool = cache_k.shape[1]
        krope_t = jnp.transpose(cache_krope[layer], (0, 2, 1))
        kidx_t = jnp.transpose(cache_kidx[layer], (0, 2, 1))
        cache_k3 = cache_k[layer].reshape(n_pool, PAGE * B_KV_HEADS, B_HEAD_DIM)
        cache_v3 = cache_v[layer].reshape(n_pool, PAGE * B_KV_HEADS, B_HEAD_DIM)

        def mixer_inputs(g, x):
            h = _modulate(g, x, norm_attn[layer], 1, 0)
            small = _proj_small(g, h, w_small, kv_norm[layer], kidx_norm[layer])
            q_all = _proj_scale(g.m, g.tm, h, w_q, q_scale, g.act_dtype)
            q_rope = _proj_rope(g, h, w_qrope, w_qrope_sw, a_scale)
            return h, small, q_all, q_rope

        h_p, sm_p, q_p, qr_p = mixer_inputs(gp, xp)
        ckv_p, kr_p, kb_p, vb_p, ki_p, wi_p, ckvb_p, krb_p, kbb_p, vbb_p, kib_p = sm_p
        kv_up = _proj_scale(mp, tm_p, ckvb_p, w_up, jnp.ones((w_up.shape[1],), F32), BF)
        t_attn = _pick(s, 512)
        oa_p = _mla_prompt(nb, s, q_p, qr_p, kv_up, krb_p, t_attn)
        ob_p = _dsa_prompt(nb, s, q_p, wi_p, kib_p, kbb_p, vbb_p, rel_bias, _pick(s, 256))

        h_s, sm_s, q_s, qr_s = mixer_inputs(gs, xs)
        ckv_s, kr_s, kb_s, vb_s, ki_s, wi_s = sm_s[:6]
        q_lat = _head_matmul(q_s, pl.BlockSpec((ms, A_NOPE), lambda h: (0, h)), w_uk_t,
                             jax.ShapeDtypeStruct((A_HEADS, ms, KV_LORA), F32),
                             pl.BlockSpec((None, ms, KV_LORA), lambda h: (h, 0, 0)))
        o_lat = _mla_sample(pt_flat, q_lat, qr_s, ckv_s, kr_s, cache_ckv[layer], krope_t,
                            ns, t_new, n_pages, g_pages)
        oa_s = _head_matmul(o_lat, pl.BlockSpec((None, ms, KV_LORA), lambda h: (h, 0, 0)), w_uv_h,
                            jax.ShapeDtypeStruct((ms, A_HEADS * A_V), F32),
                            pl.BlockSpec((ms, A_V), lambda h: (0, h)))
        n_sel_s = min(TOPK_MAX, (past + t_new) // 4)
        qcols = (A_HEADS * A_NOPE) // (B_HEADS * B_HEAD_DIM)
        qi_s = q_s[:, A_HEADS * A_NOPE + B_HEADS * B_HEAD_DIM:]
        keys_s, thr_s = _dsa_sample_index(pt_flat, qi_s, wi_s, ki_s, kidx_t, ns, t_new, n_pages, g_pages, n_sel_s)
        ob_s = _dsa_sample_attn(pt_flat, q_s, qcols, kb_s, vb_s, keys_s, thr_s, rel_bias, cache_k3, cache_v3,
                                ns, t_new, n_pages, g_pages)

        outs = []
        for g, x, h, o_a, o_b in ((gp, xp, h_p, oa_p, ob_p), (gs, xs, h_s, oa_s, ob_s)):
            merged = _merge(g, h, o_a, o_b, w_ga, w_gb, w_oa, w_ob)
            outs.append(_outproj(g, x, merged, w_o, norm_ffn[layer], wr_hi, wr_lo, b_r))
        (x1_p, h2_p, lt_p), (x1_s, h2_s, lt_s) = outs

        m_all = mp + ms
        lt = jnp.concatenate([lt_p, lt_s], axis=1)
        ids, wts = _router(lt, _pick(m_all, 1024))
        tm_e = 256
        pos, p_max, tile_exp, n_used = _sort_by_expert(ids, m_all, tm_e)
        pos_p = jnp.concatenate([pos[:mp], pos[m_all:m_all + mp]])
        pos_s = jnp.concatenate([pos[mp:m_all], pos[m_all + mp:]])
        wts_t = wts[:TOP_K_EXPERTS].T
        x_sorted = _dispatch(pos_p, h2_p, jnp.zeros((p_max, d), F32), _pick(mp, 256))
        x_sorted = _dispatch(pos_s, h2_s, x_sorted, _pick(ms, 256))
        ys = _moe(tile_exp, n_used, x_sorted, w1[layer], w3[layer], w2[layer], tm_e)
        last = layer == depth - 1
        assert last, "stacked layers would need the un-normalised residual stream"
        xp = _combine(gp, pos_p, wts_t[:mp], x1_p, ys, norm_final)
        xs = _combine(gs, pos_s, wts_t[mp:], x1_s, ys, norm_final)
        rows_p.append((ckv_p, kr_p, kb_p, vb_p, ki_p))
        rows_s.append((ckv_s, kr_s, kb_s, vb_s, ki_s))

    def stack(rows, k, shape):
        if len(rows) == 1:
            return rows[0][k].reshape((1,) + shape)
        return jnp.stack([r[k].reshape(shape) for r in rows])

    shp, shs = (nb, s), (ns, t_new)
    kv4 = (B_KV_HEADS, B_HEAD_DIM)
    return (xp.reshape(nb, s, d), xs.reshape(ns, t_new, d),
            stack(rows_p, 0, shp + (KV_LORA,)), stack(rows_p, 1, shp + (A_ROPE,)),
            stack(rows_p, 2, shp + kv4), stack(rows_p, 3, shp + kv4), stack(rows_p, 4, shp + (IDX_DIM,)),
            stack(rows_s, 0, shs + (KV_LORA,)), stack(rows_s, 1, shs + (A_ROPE,)),
            stack(rows_s, 2, shs + kv4), stack(rows_s, 3, shs + kv4), stack(rows_s, 4, shs + (IDX_DIM,)))
```
